```python
import math
import jax, jax.numpy as jnp
from jax import lax
import numpy as np

D_MODEL = 1024
BATCH = 32
SEQ = 2048
DEPTH = 2

HEAD_DIM = 64
BLOCK_Q = 128
RMS_EPS = 1e-6
NEG_INF = -1e30
TINY = 1e-30

FOX_HEADS = 4
FORGET_BIAS_INIT = 3.0
DIL_WINDOWS = (128, 512, 2048)
DIL_RATES = (1, 4, 16)
DIL_HEADS_PER_GROUP = 2
DIL_HEADS = DIL_HEADS_PER_GROUP * len(DIL_WINDOWS)
NSA_HEADS = 6
NSA_KV_GROUPS = 2
NSA_CMP_BLOCK = 32
NSA_CMP_STRIDE = 16
NSA_CMP_HIDDEN = 256
NSA_SEL_BLOCK = 64
NSA_SEL_TOPK = 8
NSA_WINDOW = 512
FORCE_SCORE = 1e6
ALIBI_HEADS = DIL_HEADS + NSA_HEADS
D_FF = 2816
CONV_WIDTH = 3

FOX_W = FOX_HEADS * HEAD_DIM
DIL_W = DIL_HEADS * HEAD_DIM
NSA_W = NSA_HEADS * HEAD_DIM
NSA_KV_W = NSA_KV_GROUPS * HEAD_DIM
IN_SPLITS = (FOX_W, FOX_W, FOX_W, FOX_HEADS,
             DIL_W, DIL_W, DIL_W,
             NSA_W, NSA_KV_W, NSA_KV_W, NSA_KV_W, NSA_KV_W, NSA_KV_W, NSA_KV_W, NSA_HEADS * 3,
             3 * D_MODEL)
D_IN = sum(IN_SPLITS)

kernel_name = "hybrid_fox_dilated_nsa_convffn"


def rms_norm(x, g):
    xf = x.astype(jnp.float32)
    y = xf * lax.rsqrt(jnp.mean(xf * xf, axis=-1, keepdims=True) + RMS_EPS)
    return (y * g.astype(jnp.float32)).astype(x.dtype)


def alibi_slopes(n):
    return jnp.exp2(-8.0 * jnp.arange(1, n + 1, dtype=jnp.float32) / n)


def masked_softmax(s, mask):
    s = jnp.where(mask, s, NEG_INF)
    m = jnp.max(s, axis=-1, keepdims=True)
    e = jnp.where(mask, jnp.exp(s - m), 0.0)
    l = jnp.maximum(jnp.sum(e, axis=-1, keepdims=True), TINY)
    return e / l, (m + jnp.log(l))[..., 0]


def fox_attention(q, k, v, f_logit):
    B_, S, H, dh = q.shape
    nb = S // BLOCK_Q
    scale = dh ** -0.5
    c = jnp.cumsum(jax.nn.log_sigmoid(f_logit.astype(jnp.float32)), axis=1)
    c = jnp.transpose(c, (0, 2, 1))
    q_blocks = q.reshape(B_, nb, BLOCK_Q, H, dh).transpose(1, 0, 2, 3, 4)
    c_blocks = c.reshape(B_, H, nb, BLOCK_Q).transpose(2, 0, 1, 3)
    kpos = jnp.arange(S)

    def body(args):
        qb, cb, i = args
        qpos = i * BLOCK_Q + jnp.arange(BLOCK_Q)
        s = jnp.einsum('bqhd,bkhd->bhqk', qb, k).astype(jnp.float32) * scale
        s = s + cb[..., :, None] - c[:, :, None, :]
        p, _ = masked_softmax(s, kpos[None, :] <= qpos[:, None])
        return jnp.einsum('bhqk,bkhd->bqhd', p.astype(v.dtype), v)

    out = lax.map(body, (q_blocks, c_blocks, jnp.arange(nb)))
    return out.transpose(1, 0, 2, 3, 4).reshape(B_, S, H * dh)


def dilated_attention(q, k, v, slopes):
    B_, S, H, dh = q.shape
    P = DIL_HEADS_PER_GROUP
    nb = S // BLOCK_Q
    scale = dh ** -0.5
    q_blocks = q.reshape(B_, nb, BLOCK_Q, H, dh).transpose(1, 0, 2, 3, 4)

    def body(args):
        qb, i = args
        qpos = i * BLOCK_Q + jnp.arange(BLOCK_Q)
        outs, lses = [], []
        for g, (w, r) in enumerate(zip(DIL_WINDOWS, DIL_RATES)):
            hs = slice(g * P, (g + 1) * P)
            offs = r * jnp.arange(w // r + 1)
            kidx = qpos[:, None] - offs[None, :]
            valid = kidx >= 0
            kidx = jnp.maximum(kidx, 0)
            kg = k[:, :, hs][:, kidx]
            vg = v[:, :, hs][:, kidx]
            s = jnp.einsum('bqhd,bqkhd->bhqk', qb[:, :, hs], kg).astype(jnp.float32) * scale
            s = s - slopes[hs][None, :, None, None] * offs.astype(jnp.float32)[None, None, None, :]
            p, lse = masked_softmax(s, valid[None, None])
            outs.append(jnp.einsum('bhqk,bqkhd->bqhd', p.astype(vg.dtype), vg))
            lses.append(lse)
        alpha = jax.nn.softmax(jnp.stack(lses, 0), axis=0)
        o = jnp.stack(outs, 0) * alpha.transpose(0, 1, 3, 2)[..., None].astype(qb.dtype)
        return o.transpose(1, 2, 0, 3, 4).reshape(B_, BLOCK_Q, H * dh)

    out = lax.map(body, (q_blocks, jnp.arange(nb)))
    return out.transpose(1, 0, 2, 3).reshape(B_, S, H * dh)


def compress_blocks(kv, pe, w1, w2):
    B_, S, G, dh = kv.shape
    n_cmp = (S - NSA_CMP_BLOCK) // NSA_CMP_STRIDE + 1
    idx = jnp.arange(n_cmp)[:, None] * NSA_CMP_STRIDE + jnp.arange(NSA_CMP_BLOCK)[None, :]
    blocks = kv[:, idx] + pe[None, None, :, None, :]
    flat = blocks.transpose(0, 1, 3, 2, 4).reshape(B_, n_cmp, G, NSA_CMP_BLOCK * dh)
    return jax.nn.gelu(flat @ w1) @ w2


def nsa_attention(q, k_cmp, v_cmp, k_sel, v_sel, k_win, v_win, gate_logits, slopes,
                  cmp_pe, k_w1, k_w2, v_w1, v_w2):
    B_, S, H, dh = q.shape
    G = NSA_KV_GROUPS
    R = H // G
    nb = S // BLOCK_Q
    scale = dh ** -0.5
    kc = compress_blocks(k_cmp, cmp_pe, k_w1, k_w2)
    vc = compress_blocks(v_cmp, cmp_pe, v_w1, v_w2)
    n_cmp = kc.shape[1]
    cmp_end = jnp.arange(n_cmp) * NSA_CMP_STRIDE + NSA_CMP_BLOCK - 1
    cmp_start = cmp_end - (NSA_CMP_BLOCK - 1)
    n_sel = S // NSA_SEL_BLOCK
    sel_idx = jnp.arange(n_sel)
    sel_start = sel_idx * NSA_SEL_BLOCK
    overlap = ((cmp_start[:, None] < sel_start[None, :] + NSA_SEL_BLOCK)
               & (cmp_end[:, None] >= sel_start[None, :])).astype(jnp.float32)
    top_k = min(NSA_SEL_TOPK, n_sel)
    n_tok = top_k * NSA_SEL_BLOCK
    slope = slopes.reshape(G, R)[None, :, :, None, None]
    ks_t = k_sel.transpose(0, 2, 1, 3)
    vs_t = v_sel.transpose(0, 2, 1, 3)
    pad = ((0, 0), (NSA_WINDOW, 0), (0, 0), (0, 0))
    kw_pad = jnp.pad(k_win, pad)
    vw_pad = jnp.pad(v_win, pad)
    gather = jax.vmap(jax.vmap(lambda t, ii: t[ii]))
    q_blocks = q.reshape(B_, nb, BLOCK_Q, G, R, dh).transpose(1, 0, 2, 3, 4, 5)
    g_blocks = jax.nn.sigmoid(gate_logits).reshape(B_, nb, BLOCK_Q, G, R, 3).transpose(1, 0, 2, 3, 4, 5)

    def body(args):
        qb, gb, i = args
        q0 = i * BLOCK_Q
        qpos = q0 + jnp.arange(BLOCK_Q)
        s = jnp.einsum('bqgrd,bcgd->bgrqc', qb, kc).astype(jnp.float32) * scale
        s = s - slope * (qpos[:, None] - cmp_end[None, :]).astype(jnp.float32)
        p_c, _ = masked_softmax(s, cmp_end[None, :] <= qpos[:, None])
        o_c = jnp.einsum('bgrqc,bcgd->bqgrd', p_c.astype(vc.dtype), vc)
        imp = jnp.einsum('bgrqc,cj->bgqj', p_c, overlap)
        qblk = qpos // NSA_SEL_BLOCK
        causal = sel_start[None, :] <= qpos[:, None]
        forced = ((sel_idx[None, :] == 0) | (sel_idx[None, :] == qblk[:, None])
                  | (sel_idx[None, :] == qblk[:, None] - 1))
        score = jnp.where(causal & forced, FORCE_SCORE, jnp.where(causal, imp, -1.0))
        _, blk = lax.top_k(score, top_k)
        blk_ok = blk * NSA_SEL_BLOCK <= qpos[None, None, :, None]
        tok = (blk[..., None] * NSA_SEL_BLOCK + jnp.arange(NSA_SEL_BLOCK)).reshape(B_, G, BLOCK_Q, n_tok)
        tok_ok = jnp.repeat(blk_ok, NSA_SEL_BLOCK, axis=-1) & (tok <= qpos[None, None, :, None])
        kg = gather(ks_t, tok)
        vg = gather(vs_t, tok)
        s = jnp.einsum('bqgrd,bgqtd->bgrqt', qb, kg).astype(jnp.float32) * scale
        s = s - slope * (qpos[None, None, :, None] - tok)[:, :, None].astype(jnp.float32)
        p_s, _ = masked_softmax(s, tok_ok[:, :, None])
        o_s = jnp.einsum('bgrqt,bgqtd->bqgrd', p_s.astype(vg.dtype), vg)
        kw = lax.dynamic_slice_in_dim(kw_pad, q0, NSA_WINDOW + BLOCK_Q, axis=1)
        vw = lax.dynamic_slice_in_dim(vw_pad, q0, NSA_WINDOW + BLOCK_Q, axis=1)
        kpos = q0 - NSA_WINDOW + jnp.arange(NSA_WINDOW + BLOCK_Q)
        diff = qpos[:, None] - kpos[None, :]
        mask_w = (kpos[None, :] >= 0) & (diff >= 0) & (diff < NSA_WINDOW)
        s = jnp.einsum('bqgrd,bkgd->bgrqk', qb, kw).astype(jnp.float32) * scale
        s = s - slope * diff.astype(jnp.float32)
        p_w, _ = masked_softmax(s, mask_w)
        o_w = jnp.einsum('bgrqk,bkgd->bqgrd', p_w.astype(vw.dtype), vw)
        gb = gb.astype(qb.dtype)
        o = gb[..., 0:1] * o_c + gb[..., 1:2] * o_s + gb[..., 2:3] * o_w
        return o.reshape(B_, BLOCK_Q, H * dh)

    out = lax.map(body, (q_blocks, g_blocks, jnp.arange(nb)))
    return out.transpose(1, 0, 2, 3).reshape(B_, S, H * dh)


def conv_ffn(h, w_up, conv_w, conv_b, w_down):
    a, b = jnp.split(h @ w_up, 2, axis=-1)
    a = lax.conv_general_dilated(a, conv_w.reshape(CONV_WIDTH, 1, D_FF), window_strides=(1,),
                                 padding=[(CONV_WIDTH - 1, 0)],
                                 dimension_numbers=('NWC', 'WIO', 'NWC'),
                                 feature_group_count=D_FF) + conv_b
    return (jax.nn.gelu(a) * b) @ w_down


def setup_inputs(seed: int = 0) -> dict:
    key = jax.random.key(seed)
    ks = jax.random.split(key, 20)
    f32 = jnp.float32

    def nrm(k, shape, scale):
        return jax.random.normal(k, shape, f32) * scale

    L = DEPTH
    fo = 3 * FOX_W
    b_in = nrm(ks[3], (L, D_IN), 0.02).at[:, fo:fo + FOX_HEADS].add(FORGET_BIAS_INIT)
    return {
        "x": nrm(ks[0], (BATCH, SEQ, D_MODEL), 1.0),
        "norm1_g": 1.0 + nrm(ks[1], (L, D_MODEL), 0.02),
        "w_in": nrm(ks[2], (L, D_MODEL, D_IN), D_MODEL ** -0.5),
        "b_in": b_in,
        "cmp_pe": nrm(ks[4], (L, NSA_CMP_BLOCK, HEAD_DIM), 0.02),
        "cmp_k_w1": nrm(ks[5], (L, NSA_CMP_BLOCK * HEAD_DIM, NSA_CMP_HIDDEN), (NSA_CMP_BLOCK * HEAD_DIM) ** -0.5),
        "cmp_k_w2": nrm(ks[6], (L, NSA_CMP_HIDDEN, HEAD_DIM), NSA_CMP_HIDDEN ** -0.5),
        "cmp_v_w1": nrm(ks[7], (L, NSA_CMP_BLOCK * HEAD_DIM, NSA_CMP_HIDDEN), (NSA_CMP_BLOCK * HEAD_DIM) ** -0.5),
        "cmp_v_w2": nrm(ks[8], (L, NSA_CMP_HIDDEN, HEAD_DIM), NSA_CMP_HIDDEN ** -0.5),
        "w_pa": nrm(ks[9], (L, FOX_W, D_MODEL), FOX_W ** -0.5),
        "w_pb": nrm(ks[10], (L, DIL_W, D_MODEL), DIL_W ** -0.5),
        "w_pc": nrm(ks[11], (L, NSA_W, D_MODEL), NSA_W ** -0.5),
        "w_o": nrm(ks[12], (L, D_MODEL, D_MODEL), D_MODEL ** -0.5),
        "norm2_g": 1.0 + nrm(ks[13], (L, D_MODEL), 0.02),
        "w_up": nrm(ks[14], (L, D_MODEL, 2 * D_FF), D_MODEL ** -0.5),
        "conv_w": nrm(ks[15], (L, CONV_WIDTH, D_FF), CONV_WIDTH ** -0.5),
        "conv_b": nrm(ks[16], (L, D_FF), 0.02),
        "w_down": nrm(ks[17], (L, D_FF, D_MODEL), D_FF ** -0.5),
        "final_g": 1.0 + nrm(ks[18], (D_MODEL,), 0.02),
    }


def reference(x, norm1_g, w_in, b_in, cmp_pe, cmp_k_w1, cmp_k_w2, cmp_v_w1, cmp_v_w2,
              w_pa, w_pb, w_pc, w_o, norm2_g, w_up, conv_w, conv_b, w_down, final_g):
    B_, S, D = x.shape
    dh = HEAD_DIM
    G = NSA_KV_GROUPS
    slopes = alibi_slopes(ALIBI_HEADS)
    dil_slopes = slopes[0::2]
    nsa_slopes = slopes[1::2]
    split_at = np.cumsum(IN_SPLITS)[:-1].tolist()
    for l in range(DEPTH):
        h = rms_norm(x, norm1_g[l])
        proj = h @ w_in[l] + b_in[l]
        (qa, ka, va, fa, qb, kb, vb, qc, kc_, vc_, ksl, vsl, kwn, vwn, gc, gates) = jnp.split(proj, split_at, axis=-1)
        y_a = fox_attention(qa.reshape(B_, S, FOX_HEADS, dh), ka.reshape(B_, S, FOX_HEADS, dh),
                            va.reshape(B_, S, FOX_HEADS, dh), fa)
        y_b = dilated_attention(qb.reshape(B_, S, DIL_HEADS, dh), kb.reshape(B_, S, DIL_HEADS, dh),
                                vb.reshape(B_, S, DIL_HEADS, dh), dil_slopes)
        y_c = nsa_attention(qc.reshape(B_, S, NSA_HEADS, dh),
                            kc_.reshape(B_, S, G, dh), vc_.reshape(B_, S, G, dh),
                            ksl.reshape(B_, S, G, dh), vsl.reshape(B_, S, G, dh),
                            kwn.reshape(B_, S, G, dh), vwn.reshape(B_, S, G, dh),
                            gc.reshape(B_, S, NSA_HEADS, 3), nsa_slopes,
                            cmp_pe[l], cmp_k_w1[l], cmp_k_w2[l], cmp_v_w1[l], cmp_v_w2[l])
        g = jax.nn.sigmoid(gates).reshape(B_, S, 3, D)
        merged = (g[:, :, 0] * (y_a @ w_pa[l]) + g[:, :, 1] * (y_b @ w_pb[l])
                  + g[:, :, 2] * (y_c @ w_pc[l]))
        x = x + merged @ w_o[l]
        x = x + conv_ffn(rms_norm(x, norm2_g[l]), w_up[l], conv_w[l], conv_b[l], w_down[l])
    return rms_norm(x, final_g)
```

```python
import functools

import numpy as np
import jax
import jax.numpy as jnp
from jax import lax
from jax.experimental import pallas as pl
from jax.experimental.pallas import tpu as pltpu

D_MODEL = 1024
SEQ = 2048
HEAD_DIM = 64
RMS_EPS = 1e-6
NEG_INF = -1e30
TINY = 1e-30

FOX_HEADS = 4
DIL_WINDOWS = (128, 512, 2048)
DIL_RATES = (1, 4, 16)
DIL_HEADS = 6
NSA_HEADS = 6
NSA_KV_GROUPS = 2
NSA_CMP_BLOCK = 32
NSA_CMP_STRIDE = 16
NSA_CMP_HIDDEN = 256
NSA_SEL_BLOCK = 64
NSA_SEL_TOPK = 8
NSA_WINDOW = 512
FORCE_SCORE = 1e6
ALIBI_HEADS = DIL_HEADS + NSA_HEADS
D_FF = 2816
CONV_WIDTH = 3

FOX_W = FOX_HEADS * HEAD_DIM
DIL_W = DIL_HEADS * HEAD_DIM
NSA_W = NSA_HEADS * HEAD_DIM
NSA_KV_W = NSA_KV_GROUPS * HEAD_DIM
N_SEL = SEQ // NSA_SEL_BLOCK
N_CMP_PAD = SEQ // NSA_CMP_STRIDE

LANES = 128
VMEM_LIMIT_BYTES = 56 * 1024 * 1024

QKV_W = 3072
COL_FOX_Q, COL_FOX_K, COL_FOX_V = 0, 256, 512
COL_DIL_Q, COL_DIL_K, COL_DIL_V = 768, 1152, 1536
COL_NSA_Q = 1920
COL_NSA_KSEL, COL_NSA_VSEL, COL_NSA_KWIN, COL_NSA_VWIN = 2304, 2432, 2560, 2688
COL_NSA_CMP = 2816
SMALL_FA = 0
SMALL_GC = 4

ROW_TILE = 512
ATT_TILE = 128
FF_CHUNK = 256
HALO = 16

F32 = jnp.float32
BF16 = jnp.bfloat16


def _alibi_slopes_np(n):
    return np.exp2(-8.0 * np.arange(1, n + 1, dtype=np.float64) / n)


_SLOPES = _alibi_slopes_np(ALIBI_HEADS)
DIL_SLOPES = [float(np.float32(s)) for s in _SLOPES[0::2]]
NSA_SLOPES = [float(np.float32(s)) for s in _SLOPES[1::2]]


def _cparams(sem):
    return pltpu.CompilerParams(dimension_semantics=sem, vmem_limit_bytes=VMEM_LIMIT_BYTES)


def _rms(x, g):
    return x * lax.rsqrt(jnp.mean(x * x, axis=-1, keepdims=True) + RMS_EPS) * g


def _gelu_tanh(x):
    return 0.5 * x * (1.0 + jnp.tanh(np.float32(np.sqrt(2.0 / np.pi)) * (x + 0.044715 * (x * x * x))))


def _dot(a, b):
    return jnp.dot(a, b, preferred_element_type=F32)


def _dot_nt(a, b):
    return lax.dot_general(a, b, (((1,), (1,)), ((), ())), preferred_element_type=F32)


def _split3(x):
    hi = x.astype(BF16)
    r = x - hi.astype(F32)
    mid = r.astype(BF16)
    lo = (r - mid.astype(F32)).astype(BF16)
    return hi, mid, lo


def _inproj_kernel(x_ref, g_ref, w_ref, b_ref, ws_ref, bs_ref, qkv_ref, small_ref):
    h = _rms(x_ref[...], g_ref[...]).astype(BF16)
    for c in range(0, QKV_W, 512):
        qkv_ref[:, c:c + 512] = (_dot(h, w_ref[:, c:c + 512]) + b_ref[:, c:c + 512]).astype(BF16)
    small_ref[...] = _dot(h, ws_ref[...]) + bs_ref[...]


def _inproj(x2, g, w, b, ws, bs):
    t = x2.shape[0]
    const = lambda i: (0, 0)
    return pl.pallas_call(
        _inproj_kernel,
        out_shape=(jax.ShapeDtypeStruct((t, QKV_W), BF16), jax.ShapeDtypeStruct((t, LANES), F32)),
        grid=(t // ROW_TILE,),
        in_specs=[
            pl.BlockSpec((ROW_TILE, D_MODEL), lambda i: (i, 0)),
            pl.BlockSpec((1, D_MODEL), const),
            pl.BlockSpec((D_MODEL, QKV_W), const),
            pl.BlockSpec((1, QKV_W), const),
            pl.BlockSpec((D_MODEL, LANES), const),
            pl.BlockSpec((1, LANES), const),
        ],
        out_specs=(pl.BlockSpec((ROW_TILE, QKV_W), lambda i: (i, 0)),
                   pl.BlockSpec((ROW_TILE, LANES), lambda i: (i, 0))),
        compiler_params=_cparams(("parallel",)),
        name="inproj",
    )(x2, g, w, b, ws, bs)


def _half_masks(shape):
    lane = lax.broadcasted_iota(jnp.int32, shape, 1)
    low = lane < HEAD_DIM
    return low, jnp.logical_not(low)


def _slot_queries(q2):
    low, high = _half_masks(q2.shape)
    zero = jnp.zeros_like(q2)
    return [jnp.where(low, q2, zero), jnp.where(high, q2, zero)]


def _sweep(qms, k_ref, v_ref, lane0, n_iter, kb_of, score_fn):
    tq = qms[0].shape[0]
    tk = ATT_TILE

    def body(i, carry):
        kb = kb_of(i)
        start = pl.multiple_of(kb * tk, tk)
        k2 = k_ref[pl.ds(start, tk), lane0:lane0 + LANES]
        v2 = v_ref[pl.ds(start, tk), lane0:lane0 + LANES]
        new = []
        for e, qm in enumerate(qms):
            m, l, acc = carry[3 * e:3 * e + 3]
            s = score_fn(e, kb, _dot_nt(qm, k2))
            m_new = jnp.maximum(m, jnp.max(s, axis=1, keepdims=True))
            p = jnp.exp(s - m_new)
            alpha = jnp.exp(m - m_new)
            l = alpha * l + jnp.sum(p, axis=1, keepdims=True)
            acc = alpha * acc + _dot(p.astype(BF16), v2)
            new += [m_new, l, acc]
        return tuple(new)

    init = []
    for _ in qms:
        init += [jnp.full((tq, 1), NEG_INF, F32), jnp.zeros((tq, 1), F32), jnp.zeros((tq, LANES), F32)]
    out = lax.fori_loop(0, n_iter, body, tuple(init))
    return [out[3 * e:3 * e + 3] for e in range(len(qms))]


def _merge_halves(lo, hi):
    low, _ = _half_masks(lo.shape)
    return jnp.where(low, lo, hi)


def _tile_dist(qi, kb):
    r = lax.broadcasted_iota(jnp.int32, (ATT_TILE, ATT_TILE), 0)
    c = lax.broadcasted_iota(jnp.int32, (ATT_TILE, ATT_TILE), 1)
    return (qi - kb) * ATT_TILE + (r - c)


def _fox_kernel(q_ref, k_ref, v_ref, small_ref, o_ref, ccol_ref, crow_ref):
    qi = pl.program_id(1)
    tq = ATT_TILE

    @pl.when(qi == 0)
    def _():
        fa = small_ref[...]
        ls = jnp.minimum(fa, 0.0) - jnp.log(1.0 + jnp.exp(-jnp.abs(fa)))
        row = lax.broadcasted_iota(jnp.int32, ls.shape, 0)
        c = ls
        sh = 1
        while sh < SEQ:
            c = c + jnp.where(row >= sh, pltpu.roll(c, sh, 0), 0.0)
            sh *= 2
        ccol_ref[...] = c
        crow_ref[...] = c.T[0:8, :]

    row0 = pl.multiple_of(qi * tq, tq)
    q = q_ref[...] * jnp.asarray(HEAD_DIM ** -0.5, BF16)
    for j in range(FOX_HEADS // 2):
        qms = _slot_queries(q[:, j * LANES:(j + 1) * LANES])
        cts = [ccol_ref[pl.ds(row0, tq), 2 * j + e:2 * j + e + 1] for e in range(2)]

        def score(e, kb, s, j=j, cts=cts):
            h = 2 * j + e
            cs = crow_ref[h:h + 1, pl.ds(pl.multiple_of(kb * ATT_TILE, ATT_TILE), ATT_TILE)]
            s = s + (cts[e] - cs)
            return jnp.where(_tile_dist(qi, kb) >= 0, s, NEG_INF)

        res = _sweep(qms, k_ref, v_ref, j * LANES, qi + 1, lambda i: qi - i, score)
        outs = [acc / jnp.maximum(l, TINY) for (_, l, acc) in res]
        o_ref[:, j * LANES:(j + 1) * LANES] = _merge_halves(outs[0], outs[1]).astype(BF16)


def _fox(qkv3, small3):
    b = qkv3.shape[0]
    nq = SEQ // ATT_TILE
    return pl.pallas_call(
        _fox_kernel,
        out_shape=jax.ShapeDtypeStruct((b, SEQ, FOX_W), BF16),
        grid=(b, nq),
        in_specs=[
            pl.BlockSpec((None, ATT_TILE, FOX_W), lambda bi, qi: (bi, qi, COL_FOX_Q // FOX_W)),
            pl.BlockSpec((None, SEQ, FOX_W), lambda bi, qi: (bi, 0, COL_FOX_K // FOX_W)),
            pl.BlockSpec((None, SEQ, FOX_W), lambda bi, qi: (bi, 0, COL_FOX_V // FOX_W)),
            pl.BlockSpec((None, SEQ, LANES), lambda bi, qi: (bi, 0, 0)),
        ],
        out_specs=pl.BlockSpec((None, ATT_TILE, FOX_W), lambda bi, qi: (bi, qi, 0)),
        scratch_shapes=[pltpu.VMEM((SEQ, LANES), F32), pltpu.VMEM((8, SEQ), F32)],
        compiler_params=_cparams(("parallel", "arbitrary")),
        name="fox_attention",
    )(qkv3, qkv3, qkv3, small3)


def _dil_kernel(q_ref, k_ref, v_ref, o_ref):
    qi = pl.program_id(1)
    q = q_ref[...] * jnp.asarray(HEAD_DIM ** -0.5, BF16)
    outs, lses = [], []
    for g, (w, r) in enumerate(zip(DIL_WINDOWS, DIL_RATES)):
        qms = _slot_queries(q[:, g * LANES:(g + 1) * LANES])

        def score(e, kb, s, g=g, w=w, r=r):
            d = _tile_dist(qi, kb)
            ok = (d >= 0) & (d <= w) & ((d & (r - 1)) == 0)
            s = s - DIL_SLOPES[2 * g + e] * d.astype(F32)
            return jnp.where(ok, s, NEG_INF)

        n_tiles = jnp.minimum(qi, w // ATT_TILE) + 1
        res = _sweep(qms, k_ref, v_ref, g * LANES, n_tiles, lambda i: qi - i, score)
        outs.append([acc / jnp.maximum(l, TINY) for (_, l, acc) in res])
        lses.append([m + jnp.log(jnp.maximum(l, TINY)) for (m, l, _) in res])
    for g in range(len(DIL_WINDOWS)):
        scaled = []
        for e in range(2):
            top = jnp.maximum(jnp.maximum(lses[0][e], lses[1][e]), lses[2][e])
            den = sum(jnp.exp(lses[gg][e] - top) for gg in range(len(DIL_WINDOWS)))
            scaled.append(outs[g][e] * (jnp.exp(lses[g][e] - top) / den))
        o_ref[:, g * LANES:(g + 1) * LANES] = _merge_halves(scaled[0], scaled[1]).astype(BF16)


def _dil(qkv3):
    b = qkv3.shape[0]
    nq = SEQ // ATT_TILE
    return pl.pallas_call(
        _dil_kernel,
        out_shape=jax.ShapeDtypeStruct((b, SEQ, DIL_W), BF16),
        grid=(b, nq),
        in_specs=[
            pl.BlockSpec((None, ATT_TILE, DIL_W), lambda bi, qi: (bi, qi, COL_DIL_Q // DIL_W)),
            pl.BlockSpec((None, SEQ, DIL_W), lambda bi, qi: (bi, 0, COL_DIL_K // DIL_W)),
            pl.BlockSpec((None, SEQ, DIL_W), lambda bi, qi: (bi, 0, COL_DIL_V // DIL_W)),
        ],
        out_specs=pl.BlockSpec((None, ATT_TILE, DIL_W), lambda bi, qi: (bi, qi, 0)),
        compiler_params=_cparams(("parallel", "arbitrary")),
        name="dilated_attention",
    )(qkv3, qkv3, qkv3)


def _cmp_kernel(xc_ref, pek_ref, pev_ref, w1k_ref, w1v_ref, w2k_ref, w2v_ref, kc_ref, vc_ref):
    half = NSA_CMP_STRIDE * HEAD_DIM
    for kind, (pe_ref, w1_ref, w2_ref, out_ref) in enumerate(
            ((pek_ref, w1k_ref, w2k_ref, kc_ref), (pev_ref, w1v_ref, w2v_ref, vc_ref))):
        cpe = _dot(pe_ref[...], w1_ref[...])[0:1, :]
        acc = jnp.zeros((N_CMP_PAD, LANES), F32)
        for g in range(NSA_KV_GROUPS):
            x = xc_ref[kind * NSA_KV_GROUPS + g]
            a = _dot(x, w1_ref[0:half, :])
            bnext = pltpu.roll(_dot(x, w1_ref[half:2 * half, :]), N_CMP_PAD - 1, 0)
            h1 = _gelu_tanh(a + bnext + cpe)
            acc = acc + _dot(h1.astype(BF16), w2_ref[g])
        out_ref[...] = acc.astype(BF16)


def _compress(xc, pek, pev, w1k, w1v, w2k, w2v):
    b = xc.shape[0]
    c2 = lambda bi: (0, 0)
    c3 = lambda bi: (0, 0, 0)
    flat = NSA_CMP_BLOCK * HEAD_DIM
    return pl.pallas_call(
        _cmp_kernel,
        out_shape=(jax.ShapeDtypeStruct((b, N_CMP_PAD, LANES), BF16),) * 2,
        grid=(b,),
        in_specs=[
            pl.BlockSpec((None, 2 * NSA_KV_GROUPS, N_CMP_PAD, NSA_CMP_STRIDE * HEAD_DIM), lambda bi: (bi, 0, 0, 0)),
            pl.BlockSpec((8, flat), c2), pl.BlockSpec((8, flat), c2),
            pl.BlockSpec((flat, NSA_CMP_HIDDEN), c2), pl.BlockSpec((flat, NSA_CMP_HIDDEN), c2),
            pl.BlockSpec((NSA_KV_GROUPS, NSA_CMP_HIDDEN, LANES), c3),
            pl.BlockSpec((NSA_KV_GROUPS, NSA_CMP_HIDDEN, LANES), c3),
        ],
        out_specs=(pl.BlockSpec((None, N_CMP_PAD, LANES), lambda bi: (bi, 0, 0)),) * 2,
        compiler_params=_cparams(("parallel",)),
        name="nsa_compress",
    )(xc, pek, pev, w1k, w1v, w2k, w2v)


def _nsa_head(j, e):
    return j + (NSA_HEADS // NSA_KV_GROUPS) * e


def _nsa_kernel(q_ref, ksl_ref, vsl_ref, kwn_ref, vwn_ref, kc_ref, vc_ref, small_ref,
                eg_ref, esel_ref, ov_ref, o_ref, seltok_ref):
    qi = pl.program_id(1)
    tq = ATT_TILE
    nslot = NSA_HEADS // 2
    q = q_ref[...] * jnp.asarray(HEAD_DIM ** -0.5, BF16)
    rowi = qi * tq + lax.broadcasted_iota(jnp.int32, (tq, LANES), 0)
    coli = lax.broadcasted_iota(jnp.int32, (tq, LANES), 1)

    sg = jax.nn.sigmoid(small_ref[...])
    ghi = sg.astype(BF16)
    glo = (sg - ghi.astype(F32)).astype(BF16)
    gexp = _dot(ghi, eg_ref[...]) + _dot(glo, eg_ref[...])

    dist_c = rowi - (coli * NSA_CMP_STRIDE + NSA_CMP_BLOCK - 1)
    valid_c = dist_c >= 0
    dist_cf = dist_c.astype(F32)
    kc = kc_ref[...]
    vc = vc_ref[...]
    psum = [jnp.zeros((tq, LANES), F32) for _ in range(NSA_KV_GROUPS)]
    o_cmp = []
    qslots = []
    for j in range(nslot):
        qms = _slot_queries(q[:, j * LANES:(j + 1) * LANES])
        qslots.append(qms)
        halves = []
        for e in range(2):
            s = _dot_nt(qms[e], kc) - NSA_SLOPES[_nsa_head(j, e)] * dist_cf
            s = jnp.where(valid_c, s, NEG_INF)
            m = jnp.max(s, axis=1, keepdims=True)
            p = jnp.where(valid_c, jnp.exp(s - m), 0.0)
            p = p / jnp.maximum(jnp.sum(p, axis=1, keepdims=True), TINY)
            psum[e] = psum[e] + p
            halves.append(_dot(p.astype(BF16), vc))
        o_cmp.append(_merge_halves(halves[0], halves[1]))

    jidx = coli & (N_SEL - 1)
    qblk = rowi >> 6
    causal = jidx <= qblk
    forced = (jidx == 0) | (jidx == qblk) | (jidx == qblk - 1)
    for g in range(NSA_KV_GROUPS):
        imp = sum(_dot(part, ov_ref[...]) for part in _split3(psum[g]))
        score = jnp.where(causal & forced, FORCE_SCORE, jnp.where(causal, imp, -1.0))
        rank = jnp.zeros((tq, LANES), jnp.int32)
        for k in range(1, N_SEL):
            other = pltpu.roll(score, k, 1)
            other_idx = pltpu.roll(jidx, k, 1)
            ahead = (other > score) | ((other == score) & (other_idx < jidx))
            rank = rank + ahead.astype(jnp.int32)
        sel = (rank < NSA_SEL_TOPK) & causal & (coli < N_SEL)
        seltok_ref[g] = _dot(jnp.where(sel, 1.0, 0.0).astype(BF16), esel_ref[...])

    for j in range(nslot):
        qms = qslots[j]

        def score_sel(e, kb, s, j=j):
            d = _tile_dist(qi, kb)
            chosen = seltok_ref[e, :, pl.ds(pl.multiple_of(kb * ATT_TILE, ATT_TILE), ATT_TILE)] > 0.5
            s = s - NSA_SLOPES[_nsa_head(j, e)] * d.astype(F32)
            return jnp.where(chosen & (d >= 0), s, NEG_INF)

        def score_win(e, kb, s, j=j):
            d = _tile_dist(qi, kb)
            s = s - NSA_SLOPES[_nsa_head(j, e)] * d.astype(F32)
            return jnp.where((d >= 0) & (d < NSA_WINDOW), s, NEG_INF)

        res_s = _sweep(qms, ksl_ref, vsl_ref, 0, qi + 1, lambda i: qi - i, score_sel)
        res_w = _sweep(qms, kwn_ref, vwn_ref, 0, jnp.minimum(qi, NSA_WINDOW // ATT_TILE) + 1,
                       lambda i: qi - i, score_win)
        o_sel = _merge_halves(*[acc / jnp.maximum(l, TINY) for (_, l, acc) in res_s])
        o_win = _merge_halves(*[acc / jnp.maximum(l, TINY) for (_, l, acc) in res_w])
        g_c = gexp[:, 0 * NSA_W + j * LANES:0 * NSA_W + (j + 1) * LANES]
        g_s = gexp[:, 1 * NSA_W + j * LANES:1 * NSA_W + (j + 1) * LANES]
        g_w = gexp[:, 2 * NSA_W + j * LANES:2 * NSA_W + (j + 1) * LANES]
        o_ref[:, j * LANES:(j + 1) * LANES] = (g_c * o_cmp[j] + g_s * o_sel + g_w * o_win).astype(BF16)


def _nsa(qkv3, kc2, vc2, small3, eg, esel, ov):
    b = qkv3.shape[0]
    nq = SEQ // ATT_TILE
    full = lambda col: pl.BlockSpec((None, SEQ, LANES), lambda bi, qi: (bi, 0, col // LANES))
    c2 = lambda bi, qi: (0, 0)
    return pl.pallas_call(
        _nsa_kernel,
        out_shape=jax.ShapeDtypeStruct((b, SEQ, NSA_W), BF16),
        grid=(b, nq),
        in_specs=[
            pl.BlockSpec((None, ATT_TILE, NSA_W), lambda bi, qi: (bi, qi, COL_NSA_Q // NSA_W)),
            full(COL_NSA_KSEL), full(COL_NSA_VSEL), full(COL_NSA_KWIN), full(COL_NSA_VWIN),
            pl.BlockSpec((None, N_CMP_PAD, LANES), lambda bi, qi: (bi, 0, 0)),
            pl.BlockSpec((None, N_CMP_PAD, LANES), lambda bi, qi: (bi, 0, 0)),
            pl.BlockSpec((None, ATT_TILE, LANES), lambda bi, qi: (bi, qi, 0)),
            pl.BlockSpec((LANES, 3 * NSA_W), c2),
            pl.BlockSpec((LANES, SEQ), c2),
            pl.BlockSpec((LANES, LANES), c2),
        ],
        out_specs=pl.BlockSpec((None, ATT_TILE, NSA_W), lambda bi, qi: (bi, qi, 0)),
        scratch_shapes=[pltpu.VMEM((NSA_KV_GROUPS, ATT_TILE, SEQ), F32)],
        compiler_params=_cparams(("parallel", "arbitrary")),
        name="nsa_attention",
    )(qkv3, qkv3, qkv3, qkv3, qkv3, kc2, vc2, small3, eg, esel, ov)


def _merge_kernel(x_ref, g_ref, ya_ref, yb_ref, yc_ref, wg_ref, bg_ref, wpa_ref, wpb_ref, wpc_ref, wo_ref, o_ref):
    x = x_ref[...]
    h = _rms(x, g_ref[...]).astype(BF16)
    merged = None
    for i, (y_ref, wp_ref) in enumerate(((ya_ref, wpa_ref), (yb_ref, wpb_ref), (yc_ref, wpc_ref))):
        cols = slice(i * D_MODEL, (i + 1) * D_MODEL)
        gate = jax.nn.sigmoid(_dot(h, wg_ref[:, cols]) + bg_ref[:, cols])
        term = gate * _dot(y_ref[...], wp_ref[...])
        merged = term if merged is None else merged + term
    o_ref[...] = x + _dot(merged.astype(BF16), wo_ref[...])


def _merge(x2, g, ya, yb, yc, wg, bg, wpa, wpb, wpc, wo):
    t = x2.shape[0]
    const = lambda i: (0, 0)
    rows = lambda w: pl.BlockSpec((ROW_TILE, w), lambda i: (i, 0))
    whole = lambda a: pl.BlockSpec(a.shape, const)
    return pl.pallas_call(
        _merge_kernel,
        out_shape=jax.ShapeDtypeStruct((t, D_MODEL), F32),
        grid=(t // ROW_TILE,),
        in_specs=[rows(D_MODEL), whole(g), rows(FOX_W), rows(DIL_W), rows(NSA_W),
                  whole(wg), whole(bg), whole(wpa), whole(wpb), whole(wpc), whole(wo)],
        out_specs=rows(D_MODEL),
        compiler_params=_cparams(("parallel",)),
        name="merge_oproj",
    )(x2, g, ya, yb, yc, wg, bg, wpa, wpb, wpc, wo)


def _ffn_kernel(x_ref, xh_ref, g_ref, wup_ref, cw_ref, cb_ref, wdn_ref, fg_ref, o_ref, h_ref, act_ref, *, final_norm):
    i = pl.program_id(0)
    x = x_ref[...]
    g = g_ref[...]
    h_ref[0:HALO, :] = _rms(xh_ref[...], g).astype(BF16)
    h_ref[HALO:, :] = _rms(x, g).astype(BF16)
    row = lax.broadcasted_iota(jnp.int32, (ROW_TILE + HALO, FF_CHUNK), 0)
    seq_start = (i % (SEQ // ROW_TILE)) == 0
    keep = jnp.logical_not((row < HALO) & seq_start)
    for c in range(0, D_FF, FF_CHUNK):
        a = jnp.where(keep, _dot(h_ref[...], wup_ref[:, c:c + FF_CHUNK]), 0.0)
        conv = (a * cw_ref[2:3, c:c + FF_CHUNK]
                + pltpu.roll(a, 1, 0) * cw_ref[1:2, c:c + FF_CHUNK]
                + pltpu.roll(a, 2, 0) * cw_ref[0:1, c:c + FF_CHUNK]
                + cb_ref[:, c:c + FF_CHUNK])
        gate = _dot(h_ref[HALO:, :], wup_ref[:, D_FF + c:D_FF + c + FF_CHUNK])
        act_ref[:, c:c + FF_CHUNK] = (_gelu_tanh(conv[HALO:, :]) * gate).astype(BF16)
    y = x + _dot(act_ref[...], wdn_ref[...])
    if final_norm:
        y = _rms(y, fg_ref[...])
    o_ref[...] = y


def _ffn(x2, g, wup, cw, cb, wdn, fg, final_norm):
    t = x2.shape[0]
    const = lambda i: (0, 0)
    whole = lambda a: pl.BlockSpec(a.shape, const)
    halo_blocks = ROW_TILE // HALO
    return pl.pallas_call(
        functools.partial(_ffn_kernel, final_norm=final_norm),
        out_shape=jax.ShapeDtypeStruct((t, D_MODEL), F32),
        grid=(t // ROW_TILE,),
        in_specs=[
            pl.BlockSpec((ROW_TILE, D_MODEL), lambda i: (i, 0)),
            pl.BlockSpec((HALO, D_MODEL), lambda i: (jnp.maximum(i * halo_blocks - 1, 0), 0)),
            whole(g),
            pl.BlockSpec(wup.shape, const, pipeline_mode=pl.Buffered(1)),
            whole(cw), whole(cb),
            pl.BlockSpec(wdn.shape, const, pipeline_mode=pl.Buffered(1)),
            whole(fg),
        ],
        out_specs=pl.BlockSpec((ROW_TILE, D_MODEL), lambda i: (i, 0)),
        scratch_shapes=[pltpu.VMEM((ROW_TILE + HALO, D_MODEL), BF16), pltpu.VMEM((ROW_TILE, D_FF), BF16)],
        compiler_params=_cparams(("parallel",)),
        name="conv_ffn",
    )(x2, x2, g, wup, cw, cb, wdn, fg)


def _nsa_slot_perm():
    cols = []
    for j in range(NSA_HEADS // 2):
        for e in range(2):
            h = _nsa_head(j, e)
            cols += list(range(h * HEAD_DIM, (h + 1) * HEAD_DIM))
    return np.asarray(cols)


def _constants():
    perm = _nsa_slot_perm()
    eg = np.zeros((LANES, 3 * NSA_W), np.float32)
    for newcol, oldcol in enumerate(perm):
        h = oldcol // HEAD_DIM
        for br in range(3):
            eg[SMALL_GC + h * 3 + br, br * NSA_W + newcol] = 1.0
    esel = np.zeros((LANES, SEQ), np.float32)
    for jb in range(N_SEL):
        esel[jb, jb * NSA_SEL_BLOCK:(jb + 1) * NSA_SEL_BLOCK] = 1.0
    ov = np.zeros((LANES, LANES), np.float32)
    n_cmp = (SEQ - NSA_CMP_BLOCK) // NSA_CMP_STRIDE + 1
    for c in range(n_cmp):
        cs, ce = c * NSA_CMP_STRIDE, c * NSA_CMP_STRIDE + NSA_CMP_BLOCK - 1
        for jb in range(N_SEL):
            if cs < jb * NSA_SEL_BLOCK + NSA_SEL_BLOCK and ce >= jb * NSA_SEL_BLOCK:
                for rep in range(LANES // N_SEL):
                    ov[c, rep * N_SEL + jb] = 1.0
    return (jnp.asarray(eg, BF16), jnp.asarray(esel, BF16), jnp.asarray(ov, BF16))


def _layer_params(l, w_in, b_in, cmp_pe, cmp_k_w1, cmp_k_w2, cmp_v_w1, cmp_v_w2, w_pc):
    o = np.cumsum([0, FOX_W, FOX_W, FOX_W, FOX_HEADS, DIL_W, DIL_W, DIL_W, NSA_W,
                   NSA_KV_W, NSA_KV_W, NSA_KV_W, NSA_KV_W, NSA_KV_W, NSA_KV_W, NSA_HEADS * 3, 3 * D_MODEL])
    (o_qa, o_ka, o_va, o_fa, o_qb, o_kb, o_vb, o_qc, o_kc, o_vc, o_ks, o_vs, o_kw, o_vw, o_gc, o_g, _) = [int(v) for v in o]
    perm = _nsa_slot_perm()
    order = np.concatenate([
        np.arange(o_qa, o_qa + 3 * FOX_W),
        np.arange(o_qb, o_qb + 3 * DIL_W),
        o_qc + perm,
        np.arange(o_ks, o_ks + 4 * NSA_KV_W),
        np.arange(o_kc, o_kc + 2 * NSA_KV_W),
    ])
    w, b = w_in[l], b_in[l]
    small_cols = np.concatenate([np.arange(o_fa, o_fa + FOX_HEADS), np.arange(o_gc, o_gc + NSA_HEADS * 3)])
    pad = LANES - small_cols.size
    p = {
        "w_qkv": w[:, order].astype(BF16),
        "b_qkv": b[order][None, :],
        "w_small": jnp.pad(w[:, small_cols], ((0, 0), (0, pad))).astype(BF16),
        "b_small": jnp.pad(b[small_cols], (0, pad))[None, :],
        "w_gate": w[:, o_g:].astype(BF16),
        "b_gate": b[o_g:][None, :],
        "w_pc": w_pc[l][perm, :].astype(BF16),
    }
    pe = cmp_pe[l].reshape(1, NSA_CMP_BLOCK * HEAD_DIM)
    p["pe"] = jnp.pad(pe, ((0, 7), (0, 0))).astype(BF16)
    for name, w2 in (("k", cmp_k_w2[l]), ("v", cmp_v_w2[l])):
        z = jnp.zeros_like(w2)
        p["w2" + name] = jnp.stack([jnp.concatenate([w2, z], 1), jnp.concatenate([z, w2], 1)]).astype(BF16)
    p["w1k"] = cmp_k_w1[l].astype(BF16)
    p["w1v"] = cmp_v_w1[l].astype(BF16)
    return p


def kernel(x, norm1_g, w_in, b_in, cmp_pe, cmp_k_w1, cmp_k_w2, cmp_v_w1, cmp_v_w2, w_pa, w_pb, w_pc, w_o,
           norm2_g, w_up, conv_w, conv_b, w_down, final_g):
    bsz, seq, d = x.shape
    assert (seq, d) == (SEQ, D_MODEL)
    depth = w_in.shape[0]
    eg, esel, ov = _constants()
    x2 = x.reshape(bsz * seq, d)
    for l in range(depth):
        p = _layer_params(l, w_in, b_in, cmp_pe, cmp_k_w1, cmp_k_w2, cmp_v_w1, cmp_v_w2, w_pc)
        g1 = norm1_g[l][None, :]
        qkv, small = _inproj(x2, g1, p["w_qkv"], p["b_qkv"], p["w_small"], p["b_small"])
        qkv3 = qkv.reshape(bsz, seq, QKV_W)
        small3 = small.reshape(bsz, seq, LANES)
        ya = _fox(qkv3, small3)
        yb = _dil(qkv3)
        xc = qkv3[:, :, COL_NSA_CMP:COL_NSA_CMP + 2 * NSA_KV_W]
        xc = xc.reshape(bsz, N_CMP_PAD, NSA_CMP_STRIDE, 2 * NSA_KV_GROUPS, HEAD_DIM)
        xc = xc.transpose(0, 3, 1, 2, 4).reshape(bsz, 2 * NSA_KV_GROUPS, N_CMP_PAD, NSA_CMP_STRIDE * HEAD_DIM)
        kc2, vc2 = _compress(xc, p["pe"], p["pe"], p["w1k"], p["w1v"], p["w2k"], p["w2v"])
        yc = _nsa(qkv3, kc2, vc2, small3, eg, esel, ov)
        x2 = _merge(x2, g1, ya.reshape(-1, FOX_W), yb.reshape(-1, DIL_W), yc.reshape(-1, NSA_W),
                    p["w_gate"], p["b_gate"], w_pa[l].astype(BF16), w_pb[l].astype(BF16), p["w_pc"],
                    w_o[l].astype(BF16))
        x2 = _ffn(x2, norm2_g[l][None, :], w_up[l].astype(BF16), conv_w[l], conv_b[l][None, :],
                  w_down[l].astype(BF16), final_g[None, :], final_norm=(l == depth - 1))
    return x2.reshape(bsz, seq, d)
```

```python
import functools

import numpy as np
import jax
import jax.numpy as jnp
from jax import lax
from jax.experimental import pallas as pl
from jax.experimental.pallas import tpu as pltpu

D_MODEL = 1024
SEQ = 2048
HEAD_DIM = 64
RMS_EPS = 1e-6
NEG_INF = -1e30
TINY = 1e-30

FOX_HEADS = 4
DIL_WINDOWS = (128, 512, 2048)
DIL_RATES = (1, 4, 16)
DIL_HEADS = 6
NSA_HEADS = 6
NSA_KV_GROUPS = 2
NSA_CMP_BLOCK = 32
NSA_CMP_STRIDE = 16
NSA_CMP_HIDDEN = 256
NSA_SEL_BLOCK = 64
NSA_SEL_TOPK = 8
NSA_WINDOW = 512
FORCE_SCORE = 1e6
ALIBI_HEADS = DIL_HEADS + NSA_HEADS
D_FF = 2816
CONV_WIDTH = 3

FOX_W = FOX_HEADS * HEAD_DIM
DIL_W = DIL_HEADS * HEAD_DIM
NSA_W = NSA_HEADS * HEAD_DIM
NSA_KV_W = NSA_KV_GROUPS * HEAD_DIM
N_SEL = SEQ // NSA_SEL_BLOCK
N_CMP_PAD = SEQ // NSA_CMP_STRIDE

LANES = 128
VMEM_LIMIT_BYTES = 56 * 1024 * 1024

QKV_W = 3072
COL_FOX_Q, COL_FOX_K, COL_FOX_V = 0, 256, 512
COL_DIL_Q, COL_DIL_K, COL_DIL_V = 768, 1152, 1536
COL_NSA_Q = 1920
COL_NSA_KSEL, COL_NSA_VSEL, COL_NSA_KWIN, COL_NSA_VWIN = 2304, 2432, 2560, 2688
COL_NSA_CMP = 2816
SMALL_FA = 0
SMALL_GC = 4

ROW_TILE = 512
FF_CHUNK = 256
HALO = 16

ATT_TQ = 256
ATT_TK = 512
WIN_SPAN = NSA_WINDOW + ATT_TQ
DIL_BAND = DIL_WINDOWS[0] // DIL_RATES[0]
DIL_SPAN = ATT_TQ + DIL_BAND

F32 = jnp.float32
BF16 = jnp.bfloat16


def _alibi_slopes_np(n):
    return np.exp2(-8.0 * np.arange(1, n + 1, dtype=np.float64) / n)


_SLOPES = _alibi_slopes_np(ALIBI_HEADS)
DIL_SLOPES = [float(np.float32(s)) for s in _SLOPES[0::2]]
NSA_SLOPES = [float(np.float32(s)) for s in _SLOPES[1::2]]


def _cparams(sem):
    return pltpu.CompilerParams(dimension_semantics=sem, vmem_limit_bytes=VMEM_LIMIT_BYTES)


def _rms(x, g):
    return x * lax.rsqrt(jnp.mean(x * x, axis=-1, keepdims=True) + RMS_EPS) * g


def _gelu_tanh(x):
    return 0.5 * x * (1.0 + jnp.tanh(np.float32(np.sqrt(2.0 / np.pi)) * (x + 0.044715 * (x * x * x))))


def _dot(a, b):
    return jnp.dot(a, b, preferred_element_type=F32)


def _dot_nt(a, b):
    return lax.dot_general(a, b, (((1,), (1,)), ((), ())), preferred_element_type=F32)


def _split3(x):
    hi = x.astype(BF16)
    r = x - hi.astype(F32)
    mid = r.astype(BF16)
    lo = (r - mid.astype(F32)).astype(BF16)
    return hi, mid, lo


def _inproj_kernel(x_ref, g_ref, w_ref, b_ref, ws_ref, bs_ref, qkv_ref, small_ref):
    h = _rms(x_ref[...], g_ref[...]).astype(BF16)
    for c in range(0, QKV_W, 512):
        qkv_ref[:, c:c + 512] = (_dot(h, w_ref[:, c:c + 512]) + b_ref[:, c:c + 512]).astype(BF16)
    small_ref[...] = _dot(h, ws_ref[...]) + bs_ref[...]


def _inproj(x2, g, w, b, ws, bs):
    t = x2.shape[0]
    const = lambda i: (0, 0)
    return pl.pallas_call(
        _inproj_kernel,
        out_shape=(jax.ShapeDtypeStruct((t, QKV_W), BF16), jax.ShapeDtypeStruct((t, LANES), F32)),
        grid=(t // ROW_TILE,),
        in_specs=[
            pl.BlockSpec((ROW_TILE, D_MODEL), lambda i: (i, 0)),
            pl.BlockSpec((1, D_MODEL), const),
            pl.BlockSpec((D_MODEL, QKV_W), const),
            pl.BlockSpec((1, QKV_W), const),
            pl.BlockSpec((D_MODEL, LANES), const),
            pl.BlockSpec((1, LANES), const),
        ],
        out_specs=(pl.BlockSpec((ROW_TILE, QKV_W), lambda i: (i, 0)),
                   pl.BlockSpec((ROW_TILE, LANES), lambda i: (i, 0))),
        compiler_params=_cparams(("parallel",)),
        name="inproj",
    )(x2, g, w, b, ws, bs)


def _own_and_ones(x, low_half):
    lane = lax.broadcasted_iota(jnp.int32, x.shape, 1)
    own = (lane < HEAD_DIM) if low_half else (lane >= HEAD_DIM)
    return jnp.where(own, x, jnp.ones_like(x))


def _own_and_zeros(x, low_half):
    lane = lax.broadcasted_iota(jnp.int32, x.shape, 1)
    own = (lane < HEAD_DIM) if low_half else (lane >= HEAD_DIM)
    return jnp.where(own, x, jnp.zeros_like(x))


def _softmax_step(chunks, off, m_old, v_tile):
    mx = chunks[0]
    for sc in chunks[1:]:
        mx = jnp.maximum(mx, sc)
    m_new = jnp.maximum(m_old, jnp.max(mx, axis=1, keepdims=True) + off)
    shift = m_new - off
    p = jnp.concatenate([jnp.exp(sc - shift).astype(BF16) for sc in chunks], axis=1)
    return m_new, jnp.exp(m_old - m_new), _dot(p, v_tile)


def _normalise(acc):
    return acc / jnp.maximum(pltpu.roll(acc, HEAD_DIM, 1), TINY)


def _fox_kernel(q_ref, k_ref, v_ref, small_ref, o_ref, ccol_ref, crow_ref, vaug_ref, qst_ref, m_ref, acc_ref):
    qi = pl.program_id(1)
    tq, tk = ATT_TQ, ATT_TK
    nslot = FOX_HEADS // 2

    @pl.when(qi == 0)
    def _():
        fa = small_ref[...]
        ls = jnp.minimum(fa, 0.0) - jnp.log(1.0 + jnp.exp(-jnp.abs(fa)))
        row = lax.broadcasted_iota(jnp.int32, ls.shape, 0)
        c = ls
        sh = 1
        while sh < SEQ:
            c = c + jnp.where(row >= sh, pltpu.roll(c, sh, 0), 0.0)
            sh *= 2
        crow_ref[...] = c.T[0:8, :]
        for h in range(FOX_HEADS):
            ccol_ref[h] = jnp.broadcast_to(c[:, h:h + 1], (SEQ, LANES))
        for j in range(nslot):
            v2 = v_ref[:, j * LANES:(j + 1) * LANES]
            vaug_ref[2 * j] = _own_and_ones(v2, True)
            vaug_ref[2 * j + 1] = _own_and_ones(v2, False)

    row0 = pl.multiple_of(qi * tq, tq)
    scale = jnp.asarray(HEAD_DIM ** -0.5, BF16)
    for j in range(nslot):
        q2 = q_ref[:, j * LANES:(j + 1) * LANES] * scale
        qst_ref[j, 0:tq, :] = _own_and_zeros(q2, True)
        qst_ref[j, tq:2 * tq, :] = _own_and_zeros(q2, False)
    m_ref[...] = jnp.full(m_ref.shape, NEG_INF, F32)
    acc_ref[...] = jnp.zeros(acc_ref.shape, F32)
    lane = lax.broadcasted_iota(jnp.int32, (tq, LANES), 1)
    rloc = lax.broadcasted_iota(jnp.int32, (tq, LANES), 0)

    def tile(kb, diag):
        start = pl.multiple_of(kb * tk, tk)
        for j in range(nslot):
            s_slot = _dot_nt(qst_ref[j], k_ref[pl.ds(start, tk), j * LANES:(j + 1) * LANES])
            for e in range(2):
                h = 2 * j + e
                chunks = []
                for ck in range(tk // LANES):
                    sc = s_slot[e * tq:(e + 1) * tq, ck * LANES:(ck + 1) * LANES]
                    sc = sc - crow_ref[h:h + 1, pl.ds(pl.multiple_of(start + ck * LANES, LANES), LANES)]
                    if diag:
                        sc = jnp.where((lane - rloc) <= (row0 - start - ck * LANES), sc, NEG_INF)
                    chunks.append(sc)
                m_new, alpha, pv = _softmax_step(chunks, ccol_ref[h, pl.ds(row0, tq), :], m_ref[h],
                                                 vaug_ref[h, pl.ds(start, tk), :])
                m_ref[h] = m_new
                acc_ref[h] = alpha * acc_ref[h] + pv

    nfull = (qi * tq) // tk

    def body(kb, carry):
        tile(kb, False)
        return carry

    lax.fori_loop(0, nfull, body, 0)
    tile(nfull, True)

    for j in range(nslot):
        o_ref[:, j * LANES:(j + 1) * LANES] = jnp.where(
            lane < HEAD_DIM, _normalise(acc_ref[2 * j]), _normalise(acc_ref[2 * j + 1])).astype(BF16)


def _fox(qkv3, small3):
    b = qkv3.shape[0]
    return pl.pallas_call(
        _fox_kernel,
        out_shape=jax.ShapeDtypeStruct((b, SEQ, FOX_W), BF16),
        grid=(b, SEQ // ATT_TQ),
        in_specs=[
            pl.BlockSpec((None, ATT_TQ, FOX_W), lambda bi, qi: (bi, qi, COL_FOX_Q // FOX_W)),
            pl.BlockSpec((None, SEQ, FOX_W), lambda bi, qi: (bi, 0, COL_FOX_K // FOX_W)),
            pl.BlockSpec((None, SEQ, FOX_W), lambda bi, qi: (bi, 0, COL_FOX_V // FOX_W)),
            pl.BlockSpec((None, SEQ, LANES), lambda bi, qi: (bi, 0, 0)),
        ],
        out_specs=pl.BlockSpec((None, ATT_TQ, FOX_W), lambda bi, qi: (bi, qi, 0)),
        scratch_shapes=[
            pltpu.VMEM((FOX_HEADS, SEQ, LANES), F32),
            pltpu.VMEM((8, SEQ), F32),
            pltpu.VMEM((FOX_HEADS, SEQ, LANES), BF16),
            pltpu.VMEM((FOX_HEADS // 2, 2 * ATT_TQ, LANES), BF16),
            pltpu.VMEM((FOX_HEADS, ATT_TQ, LANES), F32),
            pltpu.VMEM((FOX_HEADS, ATT_TQ, LANES), F32),
        ],
        compiler_params=_cparams(("parallel", "arbitrary")),
        name="fox_attention",
    )(qkv3, qkv3, qkv3, small3)


def _dil_kernel(q_ref, k_ref, v_ref, o_ref, stage_ref, qd_ref, kd_ref, vaug_ref, out32_ref, lse_ref):
    tq = ATT_TQ
    ngroup = len(DIL_RATES)
    scale = jnp.asarray(HEAD_DIM ** -0.5, BF16)

    for g, r in enumerate(DIL_RATES):
        n = SEQ // r
        cols = slice(g * LANES, (g + 1) * LANES)
        for which, src in enumerate((q_ref, k_ref, v_ref)):
            x = src[:, cols]
            if which == 0:
                x = x * scale
            if r > 1:
                stage_ref[...] = x.astype(F32)
                x = jnp.concatenate([stage_ref[pl.ds(rho, n, stride=r), :] for rho in range(r)], axis=0).astype(BF16)
            if which == 0:
                qd_ref[g, 0] = _own_and_zeros(x, True)
                qd_ref[g, 1] = _own_and_zeros(x, False)
            elif which == 1:
                kd_ref[g] = x
            else:
                vaug_ref[g, 0] = _own_and_ones(x, True)
                vaug_ref[g, 1] = _own_and_ones(x, False)

    lane = lax.broadcasted_iota(jnp.int32, (tq, LANES), 1)
    rloc = lax.broadcasted_iota(jnp.int32, (tq, LANES), 0)
    low = lane < HEAD_DIM

    for g, r in enumerate(DIL_RATES):
        n = SEQ // r
        span = DIL_SPAN if n > tq else tq

        def tile(i, carry, g=g, r=r, n=n, span=span):
            u0 = pl.multiple_of(i * tq, tq)
            if n > tq:
                cstart = (u0 // n) * n
                kstart = pl.multiple_of(cstart + jnp.maximum(u0 - cstart - DIL_BAND, 0), LANES)
            else:
                kstart = u0
            delta = u0 - kstart
            k2 = kd_ref[g, pl.ds(kstart, span), :]
            halves, lses = [], []
            for e in range(2):
                slope = DIL_SLOPES[2 * g + e] * r
                s = _dot_nt(qd_ref[g, e, pl.ds(u0, tq), :], k2)
                chunks = []
                for ck in range(span // LANES):
                    d = (rloc - lane) + (delta - ck * LANES)
                    ok = (d >= 0) & (d <= DIL_BAND)
                    if n <= tq:
                        ok = ok & ((rloc // n) == ((lane + ck * LANES) // n))
                    sc = s[:, ck * LANES:(ck + 1) * LANES] - slope * d.astype(F32)
                    chunks.append(jnp.where(ok, sc, NEG_INF))
                mx = chunks[0]
                for sc in chunks[1:]:
                    mx = jnp.maximum(mx, sc)
                m = jnp.broadcast_to(jnp.max(mx, axis=1, keepdims=True), (tq, LANES))
                p = jnp.concatenate([jnp.exp(sc - m).astype(BF16) for sc in chunks], axis=1)
                pv = _dot(p, vaug_ref[g, e, pl.ds(kstart, span), :])
                l = jnp.maximum(pltpu.roll(pv, HEAD_DIM, 1), TINY)
                halves.append(pv / l)
                lses.append(m + jnp.log(l))
            o_slot = jnp.where(low, halves[0], halves[1])
            lse_slot = jnp.where(low, lses[0], lses[1])
            if r == 1:
                out32_ref[g, pl.ds(u0, tq), :] = o_slot
                lse_ref[g, pl.ds(u0, tq), :] = lse_slot
            elif n > tq:
                rho = u0 // n
                t0 = (u0 - rho * n) * r + rho
                out32_ref[g, pl.ds(t0, tq, stride=r), :] = o_slot
                lse_ref[g, pl.ds(t0, tq, stride=r), :] = lse_slot
            else:
                for cc in range(tq // n):
                    rho = u0 // n + cc
                    out32_ref[g, pl.ds(rho, n, stride=r), :] = o_slot[cc * n:(cc + 1) * n, :]
                    lse_ref[g, pl.ds(rho, n, stride=r), :] = lse_slot[cc * n:(cc + 1) * n, :]
            return carry

        lax.fori_loop(0, SEQ // tq, tile, 0)

    top = jnp.maximum(jnp.maximum(lse_ref[0], lse_ref[1]), lse_ref[2])
    ws = [jnp.exp(lse_ref[g] - top) for g in range(ngroup)]
    den = ws[0] + ws[1] + ws[2]
    for g in range(ngroup):
        o_ref[:, g * LANES:(g + 1) * LANES] = (out32_ref[g] * (ws[g] / den)).astype(BF16)


def _dil(qkv3):
    b = qkv3.shape[0]
    ngroup = len(DIL_RATES)
    blk = lambda col: pl.BlockSpec((None, SEQ, DIL_W), lambda bi: (bi, 0, col // DIL_W))
    return pl.pallas_call(
        _dil_kernel,
        out_shape=jax.ShapeDtypeStruct((b, SEQ, DIL_W), BF16),
        grid=(b,),
        in_specs=[blk(COL_DIL_Q), blk(COL_DIL_K), blk(COL_DIL_V)],
        out_specs=pl.BlockSpec((None, SEQ, DIL_W), lambda bi: (bi, 0, 0)),
        scratch_shapes=[
            pltpu.VMEM((SEQ, LANES), F32),
            pltpu.VMEM((ngroup, 2, SEQ, LANES), BF16),
            pltpu.VMEM((ngroup, SEQ, LANES), BF16),
            pltpu.VMEM((ngroup, 2, SEQ, LANES), BF16),
            pltpu.VMEM((ngroup, SEQ, LANES), F32),
            pltpu.VMEM((ngroup, SEQ, LANES), F32),
        ],
        compiler_params=_cparams(("parallel",)),
        name="dilated_attention",
    )(qkv3, qkv3, qkv3)


def _cmp_kernel(xc_ref, pek_ref, pev_ref, w1k_ref, w1v_ref, w2k_ref, w2v_ref, kc_ref, vc_ref):
    half = NSA_CMP_STRIDE * HEAD_DIM
    for kind, (pe_ref, w1_ref, w2_ref, out_ref) in enumerate(
            ((pek_ref, w1k_ref, w2k_ref, kc_ref), (pev_ref, w1v_ref, w2v_ref, vc_ref))):
        cpe = _dot(pe_ref[...], w1_ref[...])[0:1, :]
        acc = jnp.zeros((N_CMP_PAD, LANES), F32)
        for g in range(NSA_KV_GROUPS):
            x = xc_ref[kind * NSA_KV_GROUPS + g]
            a = _dot(x, w1_ref[0:half, :])
            bnext = pltpu.roll(_dot(x, w1_ref[half:2 * half, :]), N_CMP_PAD - 1, 0)
            h1 = _gelu_tanh(a + bnext + cpe)
            acc = acc + _dot(h1.astype(BF16), w2_ref[g])
        out_ref[...] = acc.astype(BF16)


def _compress(xc, pek, pev, w1k, w1v, w2k, w2v):
    b = xc.shape[0]
    c2 = lambda bi: (0, 0)
    c3 = lambda bi: (0, 0, 0)
    flat = NSA_CMP_BLOCK * HEAD_DIM
    return pl.pallas_call(
        _cmp_kernel,
        out_shape=(jax.ShapeDtypeStruct((b, N_CMP_PAD, LANES), BF16),) * 2,
        grid=(b,),
        in_specs=[
            pl.BlockSpec((None, 2 * NSA_KV_GROUPS, N_CMP_PAD, NSA_CMP_STRIDE * HEAD_DIM), lambda bi: (bi, 0, 0, 0)),
            pl.BlockSpec((8, flat), c2), pl.BlockSpec((8, flat), c2),
            pl.BlockSpec((flat, NSA_CMP_HIDDEN), c2), pl.BlockSpec((flat, NSA_CMP_HIDDEN), c2),
            pl.BlockSpec((NSA_KV_GROUPS, NSA_CMP_HIDDEN, LANES), c3),
            pl.BlockSpec((NSA_KV_GROUPS, NSA_CMP_HIDDEN, LANES), c3),
        ],
        out_specs=(pl.BlockSpec((None, N_CMP_PAD, LANES), lambda bi: (bi, 0, 0)),) * 2,
        compiler_params=_cparams(("parallel",)),
        name="nsa_compress",
    )(xc, pek, pev, w1k, w1v, w2k, w2v)


def _nsa_head(j, e):
    return j + (NSA_HEADS // NSA_KV_GROUPS) * e


def _nsa_kernel(q_ref, ksl_ref, vsl_ref, kwn_ref, vwn_ref, kc_ref, vc_ref, small_ref,
                eg_ref, esel_ref, ov_ref, o_ref,
                vaug_ref, base_ref, qst_ref, selb_ref, m_ref, acc_ref):
    qi = pl.program_id(1)
    tq, tk = ATT_TQ, ATT_TK
    nhs = NSA_HEADS
    row0 = pl.multiple_of(qi * tq, tq)
    lane = lax.broadcasted_iota(jnp.int32, (tq, LANES), 1)
    rloc = lax.broadcasted_iota(jnp.int32, (tq, LANES), 0)
    low = lane < HEAD_DIM
    slope_of = [NSA_SLOPES[_nsa_head(hs // 2, hs % 2)] for hs in range(nhs)]

    @pl.when(qi == 0)
    def _():
        for i, src in enumerate((vsl_ref, vwn_ref)):
            vaug_ref[2 * i] = _own_and_ones(src[...], True)
            vaug_ref[2 * i + 1] = _own_and_ones(src[...], False)
        r = lax.broadcasted_iota(jnp.int32, (tq, WIN_SPAN), 0)
        c = lax.broadcasted_iota(jnp.int32, (tq, WIN_SPAN), 1)
        dist = (r - c).astype(F32)
        for hs in range(nhs):
            base_ref[hs] = -slope_of[hs] * dist

    scale = jnp.asarray(HEAD_DIM ** -0.5, BF16)
    for j in range(nhs // 2):
        q2 = q_ref[:, j * LANES:(j + 1) * LANES] * scale
        qst_ref[(2 * j) * tq:(2 * j + 1) * tq, :] = _own_and_zeros(q2, True)
        qst_ref[(2 * j + 1) * tq:(2 * j + 2) * tq, :] = _own_and_zeros(q2, False)

    sg = jax.nn.sigmoid(small_ref[...])
    ghi = sg.astype(BF16)
    glo = (sg - ghi.astype(F32)).astype(BF16)
    gexp = _dot(ghi, eg_ref[...]) + _dot(glo, eg_ref[...])

    tpos = row0 + rloc
    dist_c = tpos - (lane * NSA_CMP_STRIDE + NSA_CMP_BLOCK - 1)
    valid_c = dist_c >= 0
    dist_cf = dist_c.astype(F32)
    s_c = _dot_nt(qst_ref[...], kc_ref[...])
    psum = [jnp.zeros((tq, LANES), F32) for _ in range(NSA_KV_GROUPS)]
    o_cmp = []
    for hs in range(nhs):
        s = s_c[hs * tq:(hs + 1) * tq, :] - slope_of[hs] * dist_cf
        s = jnp.where(valid_c, s, NEG_INF)
        m = jnp.max(s, axis=1, keepdims=True)
        p = jnp.where(valid_c, jnp.exp(s - m), 0.0)
        p = p / jnp.maximum(jnp.sum(p, axis=1, keepdims=True), TINY)
        psum[hs % 2] = psum[hs % 2] + p
        o_cmp.append(_dot(p.astype(BF16), vc_ref[...]))

    jidx = lane & (N_SEL - 1)
    qblk = tpos >> 6
    causal = jidx <= qblk
    forced = (jidx == 0) | (jidx == qblk) | (jidx == qblk - 1)
    for g in range(NSA_KV_GROUPS):
        imp = sum(_dot(part, ov_ref[...]) for part in _split3(psum[g]))
        score = jnp.where(causal & forced, FORCE_SCORE, jnp.where(causal, imp, -1.0))
        rank = jnp.zeros((tq, LANES), jnp.int32)
        for k in range(1, N_SEL):
            other = pltpu.roll(score, k, 1)
            other_idx = pltpu.roll(jidx, k, 1)
            ahead = (other > score) | ((other == score) & (other_idx < jidx))
            rank = rank + ahead.astype(jnp.int32)
        sel = (rank < NSA_SEL_TOPK) & causal & (lane < N_SEL)
        chosen = _dot(jnp.where(sel, 1.0, 0.0).astype(BF16), esel_ref[...])
        selb_ref[g] = (chosen - 1.0) * (-NEG_INF)

    m_ref[...] = jnp.full(m_ref.shape, NEG_INF, F32)
    acc_ref[...] = jnp.zeros(acc_ref.shape, F32)

    def sel_tile(kb, diag):
        start = pl.multiple_of(kb * tk, tk)
        s_all = _dot_nt(qst_ref[...], ksl_ref[pl.ds(start, tk), :])
        delta = row0 - start
        for hs in range(nhs):
            g = hs % 2
            chunks = []
            for ck in range(tk // LANES):
                sc = s_all[hs * tq:(hs + 1) * tq, ck * LANES:(ck + 1) * LANES]
                sc = sc + base_ref[hs, :, ck * LANES:(ck + 1) * LANES]
                sc = sc + selb_ref[g, :, pl.ds(pl.multiple_of(start + ck * LANES, LANES), LANES)]
                if diag:
                    sc = jnp.where((lane - rloc) <= (delta - ck * LANES), sc, NEG_INF)
                chunks.append(sc)
            m_new, alpha, pv = _softmax_step(chunks, -slope_of[hs] * delta.astype(F32), m_ref[hs],
                                             vaug_ref[g, pl.ds(start, tk), :])
            m_ref[hs] = m_new
            acc_ref[hs] = alpha * acc_ref[hs] + pv

    nfull = (qi * tq) // tk

    def body(i, carry):
        sel_tile(nfull - 1 - i, False)
        return carry

    sel_tile(nfull, True)
    lax.fori_loop(0, nfull, body, 0)

    wstart = pl.multiple_of(jnp.maximum(row0 - NSA_WINDOW, 0), tq)
    delta_w = row0 - wstart
    s_w = _dot_nt(qst_ref[...], kwn_ref[pl.ds(wstart, WIN_SPAN), :])
    in_window = []
    for ck in range(WIN_SPAN // LANES):
        d = (rloc - lane) + (delta_w - ck * LANES)
        in_window.append(jnp.where((d >= 0) & (d < NSA_WINDOW), 0.0, NEG_INF))
    o_win = []
    for hs in range(nhs):
        chunks = [s_w[hs * tq:(hs + 1) * tq, ck * LANES:(ck + 1) * LANES]
                  + base_ref[hs, :, ck * LANES:(ck + 1) * LANES] + in_window[ck]
                  for ck in range(WIN_SPAN // LANES)]
        _, _, pv = _softmax_step(chunks, -slope_of[hs] * delta_w.astype(F32), jnp.full((tq, LANES), NEG_INF, F32),
                                 vaug_ref[2 + hs % 2, pl.ds(wstart, WIN_SPAN), :])
        o_win.append(pv)

    for j in range(nhs // 2):
        oc = jnp.where(low, o_cmp[2 * j], o_cmp[2 * j + 1])
        osl = jnp.where(low, _normalise(acc_ref[2 * j]), _normalise(acc_ref[2 * j + 1]))
        ow = jnp.where(low, _normalise(o_win[2 * j]), _normalise(o_win[2 * j + 1]))
        g_c = gexp[:, 0 * NSA_W + j * LANES:0 * NSA_W + (j + 1) * LANES]
        g_s = gexp[:, 1 * NSA_W + j * LANES:1 * NSA_W + (j + 1) * LANES]
        g_w = gexp[:, 2 * NSA_W + j * LANES:2 * NSA_W + (j + 1) * LANES]
        o_ref[:, j * LANES:(j + 1) * LANES] = (g_c * oc + g_s * osl + g_w * ow).astype(BF16)


def _nsa(qkv3, kc2, vc2, small3, eg, esel, ov):
    b = qkv3.shape[0]
    full = lambda col: pl.BlockSpec((None, SEQ, LANES), lambda bi, qi: (bi, 0, col // LANES))
    c2 = lambda bi, qi: (0, 0)
    return pl.pallas_call(
        _nsa_kernel,
        out_shape=jax.ShapeDtypeStruct((b, SEQ, NSA_W), BF16),
        grid=(b, SEQ // ATT_TQ),
        in_specs=[
            pl.BlockSpec((None, ATT_TQ, NSA_W), lambda bi, qi: (bi, qi, COL_NSA_Q // NSA_W)),
            full(COL_NSA_KSEL), full(COL_NSA_VSEL), full(COL_NSA_KWIN), full(COL_NSA_VWIN),
            pl.BlockSpec((None, N_CMP_PAD, LANES), lambda bi, qi: (bi, 0, 0)),
            pl.BlockSpec((None, N_CMP_PAD, LANES), lambda bi, qi: (bi, 0, 0)),
            pl.BlockSpec((None, ATT_TQ, LANES), lambda bi, qi: (bi, qi, 0)),
            pl.BlockSpec((LANES, 3 * NSA_W), c2),
            pl.BlockSpec((LANES, SEQ), c2),
            pl.BlockSpec((LANES, LANES), c2),
        ],
        out_specs=pl.BlockSpec((None, ATT_TQ, NSA_W), lambda bi, qi: (bi, qi, 0)),
        scratch_shapes=[
            pltpu.VMEM((4, SEQ, LANES), BF16),
            pltpu.VMEM((NSA_HEADS, ATT_TQ, WIN_SPAN), F32),
            pltpu.VMEM((NSA_HEADS * ATT_TQ, LANES), BF16),
            pltpu.VMEM((NSA_KV_GROUPS, ATT_TQ, SEQ), F32),
            pltpu.VMEM((NSA_HEADS, ATT_TQ, LANES), F32),
            pltpu.VMEM((NSA_HEADS, ATT_TQ, LANES), F32),
        ],
        compiler_params=_cparams(("parallel", "arbitrary")),
        name="nsa_attention",
    )(qkv3, qkv3, qkv3, qkv3, qkv3, kc2, vc2, small3, eg, esel, ov)


def _merge_kernel(x_ref, g_ref, ya_ref, yb_ref, yc_ref, wg_ref, bg_ref, wpa_ref, wpb_ref, wpc_ref, wo_ref, o_ref):
    x = x_ref[...]
    h = _rms(x, g_ref[...]).astype(BF16)
    merged = None
    for i, (y_ref, wp_ref) in enumerate(((ya_ref, wpa_ref), (yb_ref, wpb_ref), (yc_ref, wpc_ref))):
        cols = slice(i * D_MODEL, (i + 1) * D_MODEL)
        gate = jax.nn.sigmoid(_dot(h, wg_ref[:, cols]) + bg_ref[:, cols])
        term = gate * _dot(y_ref[...], wp_ref[...])
        merged = term if merged is None else merged + term
    o_ref[...] = x + _dot(merged.astype(BF16), wo_ref[...])


def _merge(x2, g, ya, yb, yc, wg, bg, wpa, wpb, wpc, wo):
    t = x2.shape[0]
    const = lambda i: (0, 0)
    rows = lambda w: pl.BlockSpec((ROW_TILE, w), lambda i: (i, 0))
    whole = lambda a: pl.BlockSpec(a.shape, const)
    return pl.pallas_call(
        _merge_kernel,
        out_shape=jax.ShapeDtypeStruct((t, D_MODEL), F32),
        grid=(t // ROW_TILE,),
        in_specs=[rows(D_MODEL), whole(g), rows(FOX_W), rows(DIL_W), rows(NSA_W),
                  whole(wg), whole(bg), whole(wpa), whole(wpb), whole(wpc), whole(wo)],
        out_specs=rows(D_MODEL),
        compiler_params=_cparams(("parallel",)),
        name="merge_oproj",
    )(x2, g, ya, yb, yc, wg, bg, wpa, wpb, wpc, wo)


def _ffn_kernel(x_ref, xh_ref, g_ref, wup_ref, cw_ref, cb_ref, wdn_ref, fg_ref, o_ref, h_ref, act_ref, *, final_norm):
    i = pl.program_id(0)
    x = x_ref[...]
    g = g_ref[...]
    h_ref[0:HALO, :] = _rms(xh_ref[...], g).astype(BF16)
    h_ref[HALO:, :] = _rms(x, g).astype(BF16)
    row = lax.broadcasted_iota(jnp.int32, (ROW_TILE + HALO, FF_CHUNK), 0)
    seq_start = (i % (SEQ // ROW_TILE)) == 0
    keep = jnp.logical_not((row < HALO) & seq_start)
    for c in range(0, D_FF, FF_CHUNK):
        a = jnp.where(keep, _dot(h_ref[...], wup_ref[:, c:c + FF_CHUNK]), 0.0)
        conv = (a * cw_ref[2:3, c:c + FF_CHUNK]
                + pltpu.roll(a, 1, 0) * cw_ref[1:2, c:c + FF_CHUNK]
                + pltpu.roll(a, 2, 0) * cw_ref[0:1, c:c + FF_CHUNK]
                + cb_ref[:, c:c + FF_CHUNK])
        gate = _dot(h_ref[HALO:, :], wup_ref[:, D_FF + c:D_FF + c + FF_CHUNK])
        act_ref[:, c:c + FF_CHUNK] = (_gelu_tanh(conv[HALO:, :]) * gate).astype(BF16)
    y = x + _dot(act_ref[...], wdn_ref[...])
    if final_norm:
        y = _rms(y, fg_ref[...])
    o_ref[...] = y


def _ffn(x2, g, wup, cw, cb, wdn, fg, final_norm):
    t = x2.shape[0]
    const = lambda i: (0, 0)
    whole = lambda a: pl.BlockSpec(a.shape, const)
    halo_blocks = ROW_TILE // HALO
    return pl.pallas_call(
        functools.partial(_ffn_kernel, final_norm=final_norm),
        out_shape=jax.ShapeDtypeStruct((t, D_MODEL), F32),
        grid=(t // ROW_TILE,),
        in_specs=[
            pl.BlockSpec((ROW_TILE, D_MODEL), lambda i: (i, 0)),
            pl.BlockSpec((HALO, D_MODEL), lambda i: (jnp.maximum(i * halo_blocks - 1, 0), 0)),
            whole(g),
            pl.BlockSpec(wup.shape, const, pipeline_mode=pl.Buffered(1)),
            whole(cw), whole(cb),
            pl.BlockSpec(wdn.shape, const, pipeline_mode=pl.Buffered(1)),
            whole(fg),
        ],
        out_specs=pl.BlockSpec((ROW_TILE, D_MODEL), lambda i: (i, 0)),
        scratch_shapes=[pltpu.VMEM((ROW_TILE + HALO, D_MODEL), BF16), pltpu.VMEM((ROW_TILE, D_FF), BF16)],
        compiler_params=_cparams(("parallel",)),
        name="conv_ffn",
    )(x2, x2, g, wup, cw, cb, wdn, fg)


def _nsa_slot_perm():
    cols = []
    for j in range(NSA_HEADS // 2):
        for e in range(2):
            h = _nsa_head(j, e)
            cols += list(range(h * HEAD_DIM, (h + 1) * HEAD_DIM))
    return np.asarray(cols)


def _constants():
    perm = _nsa_slot_perm()
    eg = np.zeros((LANES, 3 * NSA_W), np.float32)
    for newcol, oldcol in enumerate(perm):
        h = oldcol // HEAD_DIM
        for br in range(3):
            eg[SMALL_GC + h * 3 + br, br * NSA_W + newcol] = 1.0
    esel = np.zeros((LANES, SEQ), np.float32)
    for jb in range(N_SEL):
        esel[jb, jb * NSA_SEL_BLOCK:(jb + 1) * NSA_SEL_BLOCK] = 1.0
    ov = np.zeros((LANES, LANES), np.float32)
    n_cmp = (SEQ - NSA_CMP_BLOCK) // NSA_CMP_STRIDE + 1
    for c in range(n_cmp):
        cs, ce = c * NSA_CMP_STRIDE, c * NSA_CMP_STRIDE + NSA_CMP_BLOCK - 1
        for jb in range(N_SEL):
            if cs < jb * NSA_SEL_BLOCK + NSA_SEL_BLOCK and ce >= jb * NSA_SEL_BLOCK:
                for rep in range(LANES // N_SEL):
                    ov[c, rep * N_SEL + jb] = 1.0
    return (jnp.asarray(eg, BF16), jnp.asarray(esel, BF16), jnp.asarray(ov, BF16))


def _layer_params(l, w_in, b_in, cmp_pe, cmp_k_w1, cmp_k_w2, cmp_v_w1, cmp_v_w2, w_pc):
    o = np.cumsum([0, FOX_W, FOX_W, FOX_W, FOX_HEADS, DIL_W, DIL_W, DIL_W, NSA_W,
                   NSA_KV_W, NSA_KV_W, NSA_KV_W, NSA_KV_W, NSA_KV_W, NSA_KV_W, NSA_HEADS * 3, 3 * D_MODEL])
    (o_qa, o_ka, o_va, o_fa, o_qb, o_kb, o_vb, o_qc, o_kc, o_vc, o_ks, o_vs, o_kw, o_vw, o_gc, o_g, _) = [int(v) for v in o]
    perm = _nsa_slot_perm()
    order = np.concatenate([
        np.arange(o_qa, o_qa + 3 * FOX_W),
        np.arange(o_qb, o_qb + 3 * DIL_W),
        o_qc + perm,
        np.arange(o_ks, o_ks + 4 * NSA_KV_W),
        np.arange(o_kc, o_kc + 2 * NSA_KV_W),
    ])
    w, b = w_in[l], b_in[l]
    small_cols = np.concatenate([np.arange(o_fa, o_fa + FOX_HEADS), np.arange(o_gc, o_gc + NSA_HEADS * 3)])
    pad = LANES - small_cols.size
    p = {
        "w_qkv": w[:, order].astype(BF16),
        "b_qkv": b[order][None, :],
        "w_small": jnp.pad(w[:, small_cols], ((0, 0), (0, pad))).astype(BF16),
        "b_small": jnp.pad(b[small_cols], (0, pad))[None, :],
        "w_gate": w[:, o_g:].astype(BF16),
        "b_gate": b[o_g:][None, :],
        "w_pc": w_pc[l][perm, :].astype(BF16),
    }
    pe = cmp_pe[l].reshape(1, NSA_CMP_BLOCK * HEAD_DIM)
    p["pe"] = jnp.pad(pe, ((0, 7), (0, 0))).astype(BF16)
    for name, w2 in (("k", cmp_k_w2[l]), ("v", cmp_v_w2[l])):
        z = jnp.zeros_like(w2)
        p["w2" + name] = jnp.stack([jnp.concatenate([w2, z], 1), jnp.concatenate([z, w2], 1)]).astype(BF16)
    p["w1k"] = cmp_k_w1[l].astype(BF16)
    p["w1v"] = cmp_v_w1[l].astype(BF16)
    return p


def kernel(x, norm1_g, w_in, b_in, cmp_pe, cmp_k_w1, cmp_k_w2, cmp_v_w1, cmp_v_w2, w_pa, w_pb, w_pc, w_o,
           norm2_g, w_up, conv_w, conv_b, w_down, final_g):
    bsz, seq, d = x.shape
    assert (seq, d) == (SEQ, D_MODEL)
    depth = w_in.shape[0]
    eg, esel, ov = _constants()
    x2 = x.reshape(bsz * seq, d)
    for l in range(depth):
        p = _layer_params(l, w_in, b_in, cmp_pe, cmp_k_w1, cmp_k_w2, cmp_v_w1, cmp_v_w2, w_pc)
        g1 = norm1_g[l][None, :]
        qkv, small = _inproj(x2, g1, p["w_qkv"], p["b_qkv"], p["w_small"], p["b_small"])
        qkv3 = qkv.reshape(bsz, seq, QKV_W)
        small3 = small.reshape(bsz, seq, LANES)
        ya = _fox(qkv3, small3)
        yb = _dil(qkv3)
        xc = qkv3[:, :, COL_NSA_CMP:COL_NSA_CMP + 2 * NSA_KV_W]
        xc = xc.reshape(bsz, N_CMP_PAD, NSA_CMP_STRIDE, 2 * NSA_KV_GROUPS, HEAD_DIM)
        xc = xc.transpose(0, 3, 1, 2, 4).reshape(bsz, 2 * NSA_KV_GROUPS, N_CMP_PAD, NSA_CMP_STRIDE * HEAD_DIM)
        kc2, vc2 = _compress(xc, p["pe"], p["pe"], p["w1k"], p["w1v"], p["w2k"], p["w2v"])
        yc = _nsa(qkv3, kc2, vc2, small3, eg, esel, ov)
        x2 = _merge(x2, g1, ya.reshape(-1, FOX_W), yb.reshape(-1, DIL_W), yc.reshape(-1, NSA_W),
                    p["w_gate"], p["b_gate"], w_pa[l].astype(BF16), w_pb[l].astype(BF16), p["w_pc"],
                    w_o[l].astype(BF16))
        x2 = _ffn(x2, norm2_g[l][None, :], w_up[l].astype(BF16), conv_w[l], conv_b[l][None, :],
                  w_down[l].astype(BF16), final_g[None, :], final_norm=(l == depth - 1))
    return x2.reshape(bsz, seq, d)
```

```python
import functools

import numpy as np
import jax
import jax.numpy as jnp
from jax import lax
from jax.experimental import pallas as pl
from jax.experimental.pallas import tpu as pltpu

D_MODEL = 1024
SEQ = 2048
HEAD_DIM = 64
RMS_EPS = 1e-6
NEG_INF = -1e30
TINY = 1e-30

FOX_HEADS = 4
DIL_WINDOWS = (128, 512, 2048)
DIL_RATES = (1, 4, 16)
DIL_HEADS = 6
NSA_HEADS = 6
NSA_KV_GROUPS = 2
NSA_CMP_BLOCK = 32
NSA_CMP_STRIDE = 16
NSA_CMP_HIDDEN = 256
NSA_SEL_BLOCK = 64
NSA_SEL_TOPK = 8
NSA_WINDOW = 512
FORCE_SCORE = 1e6
ALIBI_HEADS = DIL_HEADS + NSA_HEADS
D_FF = 2816
CONV_WIDTH = 3

FOX_W = FOX_HEADS * HEAD_DIM
DIL_W = DIL_HEADS * HEAD_DIM
NSA_W = NSA_HEADS * HEAD_DIM
NSA_KV_W = NSA_KV_GROUPS * HEAD_DIM
N_SEL = SEQ // NSA_SEL_BLOCK
N_CMP_PAD = SEQ // NSA_CMP_STRIDE

LANES = 128
VMEM_LIMIT_BYTES = 56 * 1024 * 1024

QKV_W = 3072
COL_FOX_Q, COL_FOX_K, COL_FOX_V = 0, 256, 512
COL_DIL_Q, COL_DIL_K, COL_DIL_V = 768, 1152, 1536
COL_NSA_Q = 1920
COL_NSA_KSEL, COL_NSA_VSEL, COL_NSA_KWIN, COL_NSA_VWIN = 2304, 2432, 2560, 2688
COL_NSA_CMP = 2816
SMALL_FA = 0
SMALL_GC = 4

ROW_TILE = 512
FF_CHUNK = 256
HALO = 16

ATT_TQ = 256
ATT_TK = 512
WIN_SPAN = NSA_WINDOW + ATT_TQ
DIL_BAND = DIL_WINDOWS[0] // DIL_RATES[0]
DIL_SPAN = ATT_TQ + DIL_BAND

F32 = jnp.float32
BF16 = jnp.bfloat16


def _alibi_slopes_np(n):
    return np.exp2(-8.0 * np.arange(1, n + 1, dtype=np.float64) / n)


_SLOPES = _alibi_slopes_np(ALIBI_HEADS)
DIL_SLOPES = [float(np.float32(s)) for s in _SLOPES[0::2]]
NSA_SLOPES = [float(np.float32(s)) for s in _SLOPES[1::2]]


def _cparams(sem):
    return pltpu.CompilerParams(dimension_semantics=sem, vmem_limit_bytes=VMEM_LIMIT_BYTES)


def _rms(x, g):
    return x * lax.rsqrt(jnp.mean(x * x, axis=-1, keepdims=True) + RMS_EPS) * g


def _gelu_tanh(x):
    return 0.5 * x * (1.0 + jnp.tanh(np.float32(np.sqrt(2.0 / np.pi)) * (x + 0.044715 * (x * x * x))))


def _dot(a, b):
    return jnp.dot(a, b, preferred_element_type=F32)


def _dot_nt(a, b):
    return lax.dot_general(a, b, (((1,), (1,)), ((), ())), preferred_element_type=F32)


def _split3(x):
    hi = x.astype(BF16)
    r = x - hi.astype(F32)
    mid = r.astype(BF16)
    lo = (r - mid.astype(F32)).astype(BF16)
    return hi, mid, lo


def _inproj_kernel(x_ref, g_ref, w_ref, b_ref, ws_ref, bs_ref, qkv_ref, small_ref):
    h = _rms(x_ref[...], g_ref[...]).astype(BF16)
    for c in range(0, QKV_W, 512):
        qkv_ref[:, c:c + 512] = (_dot(h, w_ref[:, c:c + 512]) + b_ref[:, c:c + 512]).astype(BF16)
    small_ref[...] = _dot(h, ws_ref[...]) + bs_ref[...]


def _inproj(x2, g, w, b, ws, bs):
    t = x2.shape[0]
    const = lambda i: (0, 0)
    return pl.pallas_call(
        _inproj_kernel,
        out_shape=(jax.ShapeDtypeStruct((t, QKV_W), BF16), jax.ShapeDtypeStruct((t, LANES), F32)),
        grid=(t // ROW_TILE,),
        in_specs=[
            pl.BlockSpec((ROW_TILE, D_MODEL), lambda i: (i, 0)),
            pl.BlockSpec((1, D_MODEL), const),
            pl.BlockSpec((D_MODEL, QKV_W), const),
            pl.BlockSpec((1, QKV_W), const),
            pl.BlockSpec((D_MODEL, LANES), const),
            pl.BlockSpec((1, LANES), const),
        ],
        out_specs=(pl.BlockSpec((ROW_TILE, QKV_W), lambda i: (i, 0)),
                   pl.BlockSpec((ROW_TILE, LANES), lambda i: (i, 0))),
        compiler_params=_cparams(("parallel",)),
        name="inproj",
    )(x2, g, w, b, ws, bs)


def _own_and_ones(x, low_half):
    lane = lax.broadcasted_iota(jnp.int32, x.shape, 1)
    own = (lane < HEAD_DIM) if low_half else (lane >= HEAD_DIM)
    return jnp.where(own, x, jnp.ones_like(x))


def _own_and_zeros(x, low_half):
    lane = lax.broadcasted_iota(jnp.int32, x.shape, 1)
    own = (lane < HEAD_DIM) if low_half else (lane >= HEAD_DIM)
    return jnp.where(own, x, jnp.zeros_like(x))


def _softmax_step(chunks, off, m_old, v_tile):
    mx = chunks[0]
    for sc in chunks[1:]:
        mx = jnp.maximum(mx, sc)
    m_new = jnp.maximum(m_old, jnp.max(mx, axis=1, keepdims=True) + off)
    shift = m_new - off
    p = jnp.concatenate([jnp.exp(sc - shift).astype(BF16) for sc in chunks], axis=1)
    return m_new, jnp.exp(m_old - m_new), _dot(p, v_tile)


def _normalise(acc):
    return acc / jnp.maximum(pltpu.roll(acc, HEAD_DIM, 1), TINY)


def _lane_pieces(lane, first, x):
    hi, mid, lo = _split3(x)
    zero = jnp.zeros_like(x)
    out = jnp.where(lane == first, hi.astype(F32), zero)
    out = jnp.where(lane == first + 1, mid.astype(F32), out)
    return jnp.where(lane == first + 2, lo.astype(F32), out)


def _fox_kernel(q_ref, k_ref, v_ref, small_ref, o_ref, ccol_ref, kaug_ref, vaug_ref, qst_ref, m_ref, acc_ref):
    qi = pl.program_id(1)
    tq, tk = ATT_TQ, ATT_TK
    nslot = FOX_HEADS // 2

    @pl.when(qi == 0)
    def _():
        fa = small_ref[...]
        ls = jnp.minimum(fa, 0.0) - jnp.log(1.0 + jnp.exp(-jnp.abs(fa)))
        row = lax.broadcasted_iota(jnp.int32, ls.shape, 0)
        c = ls
        sh = 1
        while sh < SEQ:
            c = c + jnp.where(row >= sh, pltpu.roll(c, sh, 0), 0.0)
            sh *= 2
        lane_s = lax.broadcasted_iota(jnp.int32, (SEQ, LANES), 1)
        for h in range(FOX_HEADS):
            ccol_ref[h] = jnp.broadcast_to(c[:, h:h + 1], (SEQ, LANES))
        for j in range(nslot):
            extra = (_lane_pieces(lane_s, 0, ccol_ref[2 * j]) + _lane_pieces(lane_s, 3, ccol_ref[2 * j + 1])
                     + jnp.where((lane_s >= 6) & (lane_s < 12), 1.0, 0.0))
            kaug_ref[j] = extra.astype(BF16)
            v2 = v_ref[:, j * LANES:(j + 1) * LANES]
            vaug_ref[2 * j] = _own_and_ones(v2, True)
            vaug_ref[2 * j + 1] = _own_and_ones(v2, False)

    row0 = pl.multiple_of(qi * tq, tq)
    scale = jnp.asarray(HEAD_DIM ** -0.5, BF16)
    lane = lax.broadcasted_iota(jnp.int32, (tq, LANES), 1)
    rloc = lax.broadcasted_iota(jnp.int32, (tq, LANES), 0)
    for j in range(nslot):
        q2 = q_ref[:, j * LANES:(j + 1) * LANES] * scale
        for e in range(2):
            minus_one = jnp.where((lane >= 3 * e) & (lane < 3 * e + 3), -1.0, 0.0)
            extra = minus_one + _lane_pieces(lane, 6 + 3 * e, ccol_ref[2 * j + e, pl.ds(row0, tq), :])
            qst_ref[j, e * tq:(e + 1) * tq, 0:LANES] = _own_and_zeros(q2, e == 0)
            qst_ref[j, e * tq:(e + 1) * tq, LANES:2 * LANES] = extra.astype(BF16)
    m_ref[...] = jnp.full(m_ref.shape, NEG_INF, F32)
    acc_ref[...] = jnp.zeros(acc_ref.shape, F32)

    def tile(kb, diag):
        start = pl.multiple_of(kb * tk, tk)
        for j in range(nslot):
            keys = jnp.concatenate([k_ref[pl.ds(start, tk), j * LANES:(j + 1) * LANES],
                                    kaug_ref[j, pl.ds(start, tk), :]], axis=1)
            s_slot = _dot_nt(qst_ref[j], keys)
            for e in range(2):
                h = 2 * j + e
                chunks = []
                for ck in range(tk // LANES):
                    sc = s_slot[e * tq:(e + 1) * tq, ck * LANES:(ck + 1) * LANES]
                    if diag:
                        sc = jnp.where((lane - rloc) <= (row0 - start - ck * LANES), sc, NEG_INF)
                    chunks.append(sc)
                m_new, alpha, pv = _softmax_step(chunks, 0.0, m_ref[h], vaug_ref[h, pl.ds(start, tk), :])
                m_ref[h] = m_new
                acc_ref[h] = alpha * acc_ref[h] + pv

    nfull = (qi * tq) // tk

    def body(kb, carry):
        tile(kb, False)
        return carry

    lax.fori_loop(0, nfull, body, 0)
    tile(nfull, True)

    for j in range(nslot):
        o_ref[:, j * LANES:(j + 1) * LANES] = jnp.where(
            lane < HEAD_DIM, _normalise(acc_ref[2 * j]), _normalise(acc_ref[2 * j + 1])).astype(BF16)


def _fox(qkv3, small3):
    b = qkv3.shape[0]
    return pl.pallas_call(
        _fox_kernel,
        out_shape=jax.ShapeDtypeStruct((b, SEQ, FOX_W), BF16),
        grid=(b, SEQ // ATT_TQ),
        in_specs=[
            pl.BlockSpec((None, ATT_TQ, FOX_W), lambda bi, qi: (bi, qi, COL_FOX_Q // FOX_W)),
            pl.BlockSpec((None, SEQ, FOX_W), lambda bi, qi: (bi, 0, COL_FOX_K // FOX_W)),
            pl.BlockSpec((None, SEQ, FOX_W), lambda bi, qi: (bi, 0, COL_FOX_V // FOX_W)),
            pl.BlockSpec((None, SEQ, LANES), lambda bi, qi: (bi, 0, 0)),
        ],
        out_specs=pl.BlockSpec((None, ATT_TQ, FOX_W), lambda bi, qi: (bi, qi, 0)),
        scratch_shapes=[
            pltpu.VMEM((FOX_HEADS, SEQ, LANES), F32),
            pltpu.VMEM((FOX_HEADS // 2, SEQ, LANES), BF16),
            pltpu.VMEM((FOX_HEADS, SEQ, LANES), BF16),
            pltpu.VMEM((FOX_HEADS // 2, 2 * ATT_TQ, 2 * LANES), BF16),
            pltpu.VMEM((FOX_HEADS, ATT_TQ, LANES), F32),
            pltpu.VMEM((FOX_HEADS, ATT_TQ, LANES), F32),
        ],
        compiler_params=_cparams(("parallel", "arbitrary")),
        name="fox_attention",
    )(qkv3, qkv3, qkv3, small3)


def _dil_bias(g, e, delta):
    tq = ATT_TQ
    r = DIL_RATES[g]
    n = SEQ // r
    span = DIL_SPAN if n > tq else tq
    row = lax.broadcasted_iota(jnp.int32, (tq, span), 0)
    col = lax.broadcasted_iota(jnp.int32, (tq, span), 1)
    d = (row - col) + delta
    ok = (d >= 0) & (d <= DIL_BAND)
    if n <= tq:
        ok = ok & ((row // n) == (col // n))
    return jnp.where(ok, (-DIL_SLOPES[2 * g + e] * r) * d.astype(F32), NEG_INF)


def _dil_kernel(q_ref, k_ref, v_ref, o_ref, stage_ref, qd_ref, kd_ref, vaug_ref, out32_ref, lse_ref,
                bias_ref, bias_last_ref):
    tq = ATT_TQ
    ngroup = len(DIL_RATES)
    scale = jnp.asarray(HEAD_DIM ** -0.5, BF16)

    @pl.when(pl.program_id(0) == 0)
    def _():
        for g in range(ngroup - 1):
            for e in range(2):
                bias_ref[g, e, 0] = _dil_bias(g, e, 0)
                bias_ref[g, e, 1] = _dil_bias(g, e, DIL_BAND)
        for e in range(2):
            bias_last_ref[e] = _dil_bias(ngroup - 1, e, 0)

    for g, r in enumerate(DIL_RATES):
        n = SEQ // r
        cols = slice(g * LANES, (g + 1) * LANES)
        for which, src in enumerate((q_ref, k_ref, v_ref)):
            x = src[:, cols]
            if which == 0:
                x = x * scale
            if r > 1:
                stage_ref[...] = x.astype(F32)
                x = jnp.concatenate([stage_ref[pl.ds(rho, n, stride=r), :] for rho in range(r)], axis=0).astype(BF16)
            if which == 0:
                qd_ref[g, 0] = _own_and_zeros(x, True)
                qd_ref[g, 1] = _own_and_zeros(x, False)
            elif which == 1:
                kd_ref[g] = x
            else:
                vaug_ref[g, 0] = _own_and_ones(x, True)
                vaug_ref[g, 1] = _own_and_ones(x, False)

    low = lax.broadcasted_iota(jnp.int32, (tq, LANES), 1) < HEAD_DIM
    assert all(SEQ // r > tq for r in DIL_RATES[:-1]) and SEQ // DIL_RATES[-1] <= tq

    def tile(i, carry):
        u0 = pl.multiple_of(i * tq, tq)
        for g, r in enumerate(DIL_RATES):
            n = SEQ // r
            span = DIL_SPAN if n > tq else tq
            if n > tq:
                cstart = (u0 // n) * n
                kstart = pl.multiple_of(cstart + jnp.maximum(u0 - cstart - DIL_BAND, 0), LANES)
            else:
                kstart = u0
            later = jnp.minimum(u0 - kstart, 1)
            k2 = kd_ref[g, pl.ds(kstart, span), :]
            halves, lses = [], []
            for e in range(2):
                s = _dot_nt(qd_ref[g, e, pl.ds(u0, tq), :], k2)
                chunks = []
                for ck in range(span // LANES):
                    cols = slice(ck * LANES, (ck + 1) * LANES)
                    bias = bias_ref[g, e, later, :, cols] if n > tq else bias_last_ref[e, :, cols]
                    chunks.append(s[:, cols] + bias)
                mx = chunks[0]
                for sc in chunks[1:]:
                    mx = jnp.maximum(mx, sc)
                m = jnp.broadcast_to(jnp.max(mx, axis=1, keepdims=True), (tq, LANES))
                p = jnp.concatenate([jnp.exp(sc - m).astype(BF16) for sc in chunks], axis=1)
                pv = _dot(p, vaug_ref[g, e, pl.ds(kstart, span), :])
                l = jnp.maximum(pltpu.roll(pv, HEAD_DIM, 1), TINY)
                halves.append(pv / l)
                lses.append(m + jnp.log(l))
            o_slot = jnp.where(low, halves[0], halves[1])
            lse_slot = jnp.where(low, lses[0], lses[1])
            if r == 1:
                out32_ref[g, pl.ds(u0, tq), :] = o_slot
                lse_ref[g, pl.ds(u0, tq), :] = lse_slot
            elif n > tq:
                rho = u0 // n
                t0 = (u0 - rho * n) * r + rho
                out32_ref[g, pl.ds(t0, tq, stride=r), :] = o_slot
                lse_ref[g, pl.ds(t0, tq, stride=r), :] = lse_slot
            else:
                for cc in range(tq // n):
                    rho = u0 // n + cc
                    out32_ref[g, pl.ds(rho, n, stride=r), :] = o_slot[cc * n:(cc + 1) * n, :]
                    lse_ref[g, pl.ds(rho, n, stride=r), :] = lse_slot[cc * n:(cc + 1) * n, :]
        return carry

    lax.fori_loop(0, SEQ // tq, tile, 0)

    top = jnp.maximum(jnp.maximum(lse_ref[0], lse_ref[1]), lse_ref[2])
    ws = [jnp.exp(lse_ref[g] - top) for g in range(ngroup)]
    den = ws[0] + ws[1] + ws[2]
    for g in range(ngroup):
        o_ref[:, g * LANES:(g + 1) * LANES] = (out32_ref[g] * (ws[g] / den)).astype(BF16)


def _dil(qkv3):
    b = qkv3.shape[0]
    ngroup = len(DIL_RATES)
    blk = lambda col: pl.BlockSpec((None, SEQ, DIL_W), lambda bi: (bi, 0, col // DIL_W))
    return pl.pallas_call(
        _dil_kernel,
        out_shape=jax.ShapeDtypeStruct((b, SEQ, DIL_W), BF16),
        grid=(b,),
        in_specs=[blk(COL_DIL_Q), blk(COL_DIL_K), blk(COL_DIL_V)],
        out_specs=pl.BlockSpec((None, SEQ, DIL_W), lambda bi: (bi, 0, 0)),
        scratch_shapes=[
            pltpu.VMEM((SEQ, LANES), F32),
            pltpu.VMEM((ngroup, 2, SEQ, LANES), BF16),
            pltpu.VMEM((ngroup, SEQ, LANES), BF16),
            pltpu.VMEM((ngroup, 2, SEQ, LANES), BF16),
            pltpu.VMEM((ngroup, SEQ, LANES), F32),
            pltpu.VMEM((ngroup, SEQ, LANES), F32),
            pltpu.VMEM((ngroup - 1, 2, 2, ATT_TQ, DIL_SPAN), F32),
            pltpu.VMEM((2, ATT_TQ, ATT_TQ), F32),
        ],
        compiler_params=_cparams(("arbitrary",)),
        name="dilated_attention",
    )(qkv3, qkv3, qkv3)


PE_ROWS = 16


def _cmp_kernel(x_ref, pe_ref, w1ak_ref, w1bk_ref, w2k_ref, w1av_ref, w1bv_ref, w2v_ref, kc_ref, vc_ref,
                stage_ref, xp_ref):
    xp_ref[N_CMP_PAD:, :] = pe_ref[...]
    for kind, (w1a_ref, w1b_ref, w2_ref, out_ref) in enumerate(
            ((w1ak_ref, w1bk_ref, w2k_ref, kc_ref), (w1av_ref, w1bv_ref, w2v_ref, vc_ref))):
        stage_ref[...] = x_ref[:, kind * LANES:(kind + 1) * LANES].astype(F32)
        for m in range(NSA_CMP_STRIDE):
            xp_ref[0:N_CMP_PAD, m * LANES:(m + 1) * LANES] = stage_ref[
                pl.ds(m, N_CMP_PAD, stride=NSA_CMP_STRIDE), :].astype(BF16)
        xp = xp_ref[...]
        a = _dot(xp, w1a_ref[...])
        b = _dot(xp, w1b_ref[...])
        cpe = a[N_CMP_PAD:N_CMP_PAD + 1, :] + b[N_CMP_PAD + 1:N_CMP_PAD + 2, :]
        bnext = pltpu.roll(b[0:N_CMP_PAD, :], N_CMP_PAD - 1, 0)
        h1 = _gelu_tanh(a[0:N_CMP_PAD, :] + bnext + cpe)
        out_ref[...] = _dot(h1.astype(BF16), w2_ref[...]).astype(BF16)


def _compress(qkv3, pe2, w1ak, w1bk, w2k, w1av, w1bv, w2v):
    b = qkv3.shape[0]
    whole = lambda a: pl.BlockSpec(a.shape, lambda bi: (0, 0))
    cmp_w = 2 * NSA_KV_W
    return pl.pallas_call(
        _cmp_kernel,
        out_shape=(jax.ShapeDtypeStruct((b, N_CMP_PAD, LANES), BF16),) * 2,
        grid=(b,),
        in_specs=[pl.BlockSpec((None, SEQ, cmp_w), lambda bi: (bi, 0, COL_NSA_CMP // cmp_w)),
                  whole(pe2), whole(w1ak), whole(w1bk), whole(w2k), whole(w1av), whole(w1bv), whole(w2v)],
        out_specs=(pl.BlockSpec((None, N_CMP_PAD, LANES), lambda bi: (bi, 0, 0)),) * 2,
        scratch_shapes=[pltpu.VMEM((SEQ, LANES), F32),
                        pltpu.VMEM((N_CMP_PAD + PE_ROWS, NSA_CMP_STRIDE * LANES), BF16)],
        compiler_params=_cparams(("parallel",)),
        name="nsa_compress",
    )(qkv3, pe2, w1ak, w1bk, w2k, w1av, w1bv, w2v)


def _nsa_head(j, e):
    return j + (NSA_HEADS // NSA_KV_GROUPS) * e


XL_POS = 2 * N_SEL


def _nsa_kernel(q_ref, ksl_ref, vsl_ref, kwn_ref, vwn_ref, kc_ref, vc_ref, small_ref,
                eg_ref, ovt_ref, kx_ref, qx_ref, o_ref,
                vaug_ref, qst_ref, score_ref, selp_ref, m_ref, acc_ref):
    qi = pl.program_id(1)
    tq, tk = ATT_TQ, ATT_TK
    nhs = NSA_HEADS
    row0 = pl.multiple_of(qi * tq, tq)
    lane = lax.broadcasted_iota(jnp.int32, (tq, LANES), 1)
    rloc = lax.broadcasted_iota(jnp.int32, (tq, LANES), 0)
    low = lane < HEAD_DIM
    slope_of = [NSA_SLOPES[_nsa_head(hs // 2, hs % 2)] for hs in range(nhs)]

    @pl.when(qi == 0)
    def _():
        for i, src in enumerate((vsl_ref, vwn_ref)):
            vaug_ref[2 * i] = _own_and_ones(src[...], True)
            vaug_ref[2 * i + 1] = _own_and_ones(src[...], False)

    scale = jnp.asarray(HEAD_DIM ** -0.5, BF16)
    for j in range(nhs // 2):
        q2 = q_ref[:, j * LANES:(j + 1) * LANES] * scale
        for e in range(2):
            hs = 2 * j + e
            qst_ref[hs * tq:(hs + 1) * tq, 0:LANES] = _own_and_zeros(q2, e == 0)
            qst_ref[hs * tq:(hs + 1) * tq, LANES:2 * LANES] = jnp.broadcast_to(
                qx_ref[hs:hs + 1, :], (tq, LANES)).astype(BF16)

    sg = jax.nn.sigmoid(small_ref[...])
    ghi = sg.astype(BF16)
    glo = (sg - ghi.astype(F32)).astype(BF16)
    gexp = _dot(ghi, eg_ref[...]) + _dot(glo, eg_ref[...])

    tpos = row0 + rloc
    tposf = tpos.astype(F32)
    row_off = [-slope_of[hs] * tposf for hs in range(nhs)]

    wstart = pl.multiple_of(jnp.maximum(row0 - NSA_WINDOW, 0), tq)
    delta_w = row0 - wstart
    keys_w = jnp.concatenate([kwn_ref[pl.ds(wstart, WIN_SPAN), :], kx_ref[pl.ds(wstart, WIN_SPAN), :]], axis=1)
    s_w = _dot_nt(qst_ref[...], keys_w)
    in_window = []
    for ck in range(WIN_SPAN // LANES):
        d = (rloc - lane) + (delta_w - ck * LANES)
        in_window.append(jnp.where((d >= 0) & (d < NSA_WINDOW), 0.0, NEG_INF))
    o_win = []
    for hs in range(nhs):
        chunks = [s_w[hs * tq:(hs + 1) * tq, ck * LANES:(ck + 1) * LANES] + in_window[ck]
                  for ck in range(WIN_SPAN // LANES)]
        _, _, pv = _softmax_step(chunks, row_off[hs], jnp.full((tq, LANES), NEG_INF, F32),
                                 vaug_ref[2 + hs % 2, pl.ds(wstart, WIN_SPAN), :])
        o_win.append(pv)

    dist_c = tpos - (lane * NSA_CMP_STRIDE + NSA_CMP_BLOCK - 1)
    valid_c = dist_c >= 0
    dist_cf = dist_c.astype(F32)
    s_c = _dot_nt(qst_ref[:, 0:LANES], kc_ref[...])
    psum = [jnp.zeros((tq, LANES), F32) for _ in range(NSA_KV_GROUPS)]
    o_cmp = []
    for hs in range(nhs):
        s = s_c[hs * tq:(hs + 1) * tq, :] - slope_of[hs] * dist_cf
        s = jnp.where(valid_c, s, NEG_INF)
        m = jnp.max(s, axis=1, keepdims=True)
        p = jnp.where(valid_c, jnp.exp(s - m), 0.0)
        p = p / jnp.maximum(jnp.sum(p, axis=1, keepdims=True), TINY)
        psum[hs % 2] = psum[hs % 2] + p
        o_cmp.append(_dot(p.astype(BF16), vc_ref[...]))

    jrow = lax.broadcasted_iota(jnp.int32, (N_SEL, tq), 0)
    qblk = (row0 + lax.broadcasted_iota(jnp.int32, (N_SEL, tq), 1)) >> 6
    causal = jrow <= qblk
    forced = (jrow == 0) | (jrow == qblk) | (jrow == qblk - 1)
    selp_ref[NSA_KV_GROUPS * N_SEL:, :] = jnp.zeros((LANES - NSA_KV_GROUPS * N_SEL, tq), F32)
    for g in range(NSA_KV_GROUPS):
        imp = sum(_dot_nt(ovt_ref[...], part) for part in _split3(psum[g]))
        score = jnp.where(causal & forced, FORCE_SCORE, jnp.where(causal, imp, -1.0))
        score_ref[g] = score
        rank = jnp.zeros((N_SEL, tq), jnp.int32)
        for jp in range(N_SEL):
            other = score_ref[g, jp:jp + 1, :]
            ahead = (other > score) | ((other == score) & (jp < jrow))
            rank = rank + ahead.astype(jnp.int32)
        sel = (rank < NSA_SEL_TOPK) & causal
        selp_ref[g * N_SEL:(g + 1) * N_SEL, :] = jnp.where(sel, 0.0, NEG_INF)
    sel_t = selp_ref[...].T
    for hs in range(nhs):
        g = hs % 2
        mine = (lane >= g * N_SEL) & (lane < (g + 1) * N_SEL)
        extra = jnp.where(mine, sel_t, 0.0) + qx_ref[hs:hs + 1, :]
        qst_ref[hs * tq:(hs + 1) * tq, LANES:2 * LANES] = extra.astype(BF16)

    m_ref[...] = jnp.full(m_ref.shape, NEG_INF, F32)
    acc_ref[...] = jnp.zeros(acc_ref.shape, F32)

    def sel_tile(kb, diag):
        start = pl.multiple_of(kb * tk, tk)
        keys = jnp.concatenate([ksl_ref[pl.ds(start, tk), :], kx_ref[pl.ds(start, tk), :]], axis=1)
        s_all = _dot_nt(qst_ref[...], keys)
        for hs in range(nhs):
            chunks = []
            for ck in range(tk // LANES):
                sc = s_all[hs * tq:(hs + 1) * tq, ck * LANES:(ck + 1) * LANES]
                if diag:
                    sc = jnp.where((lane - rloc) <= (row0 - start - ck * LANES), sc, NEG_INF)
                chunks.append(sc)
            m_new, alpha, pv = _softmax_step(chunks, row_off[hs], m_ref[hs], vaug_ref[hs % 2, pl.ds(start, tk), :])
            m_ref[hs] = m_new
            acc_ref[hs] = alpha * acc_ref[hs] + pv

    nfull = (qi * tq) // tk

    def body(i, carry):
        sel_tile(nfull - 1 - i, False)
        return carry

    sel_tile(nfull, True)
    lax.fori_loop(0, nfull, body, 0)

    for j in range(nhs // 2):
        oc = jnp.where(low, o_cmp[2 * j], o_cmp[2 * j + 1])
        osl = jnp.where(low, _normalise(acc_ref[2 * j]), _normalise(acc_ref[2 * j + 1]))
        ow = jnp.where(low, _normalise(o_win[2 * j]), _normalise(o_win[2 * j + 1]))
        g_c = gexp[:, 0 * NSA_W + j * LANES:0 * NSA_W + (j + 1) * LANES]
        g_s = gexp[:, 1 * NSA_W + j * LANES:1 * NSA_W + (j + 1) * LANES]
        g_w = gexp[:, 2 * NSA_W + j * LANES:2 * NSA_W + (j + 1) * LANES]
        o_ref[:, j * LANES:(j + 1) * LANES] = (g_c * oc + g_s * osl + g_w * ow).astype(BF16)


def _nsa(qkv3, kc2, vc2, small3, eg, ovt, kx, qx):
    b = qkv3.shape[0]
    full = lambda col: pl.BlockSpec((None, SEQ, LANES), lambda bi, qi: (bi, 0, col // LANES))
    c2 = lambda bi, qi: (0, 0)
    return pl.pallas_call(
        _nsa_kernel,
        out_shape=jax.ShapeDtypeStruct((b, SEQ, NSA_W), BF16),
        grid=(b, SEQ // ATT_TQ),
        in_specs=[
            pl.BlockSpec((None, ATT_TQ, NSA_W), lambda bi, qi: (bi, qi, COL_NSA_Q // NSA_W)),
            full(COL_NSA_KSEL), full(COL_NSA_VSEL), full(COL_NSA_KWIN), full(COL_NSA_VWIN),
            pl.BlockSpec((None, N_CMP_PAD, LANES), lambda bi, qi: (bi, 0, 0)),
            pl.BlockSpec((None, N_CMP_PAD, LANES), lambda bi, qi: (bi, 0, 0)),
            pl.BlockSpec((None, ATT_TQ, LANES), lambda bi, qi: (bi, qi, 0)),
            pl.BlockSpec((LANES, 3 * NSA_W), c2),
            pl.BlockSpec((N_SEL, LANES), c2),
            pl.BlockSpec((SEQ, LANES), c2),
            pl.BlockSpec((8, LANES), c2),
        ],
        out_specs=pl.BlockSpec((None, ATT_TQ, NSA_W), lambda bi, qi: (bi, qi, 0)),
        scratch_shapes=[
            pltpu.VMEM((4, SEQ, LANES), BF16),
            pltpu.VMEM((NSA_HEADS * ATT_TQ, 2 * LANES), BF16),
            pltpu.VMEM((NSA_KV_GROUPS, N_SEL, ATT_TQ), F32),
            pltpu.VMEM((LANES, ATT_TQ), F32),
            pltpu.VMEM((NSA_HEADS, ATT_TQ, LANES), F32),
            pltpu.VMEM((NSA_HEADS, ATT_TQ, LANES), F32),
        ],
        compiler_params=_cparams(("parallel", "arbitrary")),
        name="nsa_attention",
    )(qkv3, qkv3, qkv3, qkv3, qkv3, kc2, vc2, small3, eg, ovt, kx, qx)


def _merge_kernel(x_ref, g_ref, ya_ref, yb_ref, yc_ref, wg_ref, bg_ref, wpa_ref, wpb_ref, wpc_ref, wo_ref, o_ref):
    x = x_ref[...]
    h = _rms(x, g_ref[...]).astype(BF16)
    merged = None
    for i, (y_ref, wp_ref) in enumerate(((ya_ref, wpa_ref), (yb_ref, wpb_ref), (yc_ref, wpc_ref))):
        cols = slice(i * D_MODEL, (i + 1) * D_MODEL)
        gate = jax.nn.sigmoid(_dot(h, wg_ref[:, cols]) + bg_ref[:, cols])
        term = gate * _dot(y_ref[...], wp_ref[...])
        merged = term if merged is None else merged + term
    o_ref[...] = x + _dot(merged.astype(BF16), wo_ref[...])


def _merge(x2, g, ya, yb, yc, wg, bg, wpa, wpb, wpc, wo):
    t = x2.shape[0]
    const = lambda i: (0, 0)
    rows = lambda w: pl.BlockSpec((ROW_TILE, w), lambda i: (i, 0))
    whole = lambda a: pl.BlockSpec(a.shape, const)
    return pl.pallas_call(
        _merge_kernel,
        out_shape=jax.ShapeDtypeStruct((t, D_MODEL), F32),
        grid=(t // ROW_TILE,),
        in_specs=[rows(D_MODEL), whole(g), rows(FOX_W), rows(DIL_W), rows(NSA_W),
                  whole(wg), whole(bg), whole(wpa), whole(wpb), whole(wpc), whole(wo)],
        out_specs=rows(D_MODEL),
        compiler_params=_cparams(("parallel",)),
        name="merge_oproj",
    )(x2, g, ya, yb, yc, wg, bg, wpa, wpb, wpc, wo)


def _ffn_kernel(x_ref, xh_ref, g_ref, wup_ref, cw_ref, cb_ref, wdn_ref, fg_ref, o_ref, h_ref, act_ref, *, final_norm):
    i = pl.program_id(0)
    x = x_ref[...]
    g = g_ref[...]
    h_ref[0:HALO, :] = _rms(xh_ref[...], g).astype(BF16)
    h_ref[HALO:, :] = _rms(x, g).astype(BF16)
    row = lax.broadcasted_iota(jnp.int32, (ROW_TILE + HALO, FF_CHUNK), 0)
    seq_start = (i % (SEQ // ROW_TILE)) == 0
    keep = jnp.logical_not((row < HALO) & seq_start)
    for c in range(0, D_FF, FF_CHUNK):
        a = jnp.where(keep, _dot(h_ref[...], wup_ref[:, c:c + FF_CHUNK]), 0.0)
        conv = (a * cw_ref[2:3, c:c + FF_CHUNK]
                + pltpu.roll(a, 1, 0) * cw_ref[1:2, c:c + FF_CHUNK]
                + pltpu.roll(a, 2, 0) * cw_ref[0:1, c:c + FF_CHUNK]
                + cb_ref[:, c:c + FF_CHUNK])
        gate = _dot(h_ref[HALO:, :], wup_ref[:, D_FF + c:D_FF + c + FF_CHUNK])
        act_ref[:, c:c + FF_CHUNK] = (_gelu_tanh(conv[HALO:, :]) * gate).astype(BF16)
    y = x + _dot(act_ref[...], wdn_ref[...])
    if final_norm:
        y = _rms(y, fg_ref[...])
    o_ref[...] = y


def _ffn(x2, g, wup, cw, cb, wdn, fg, final_norm):
    t = x2.shape[0]
    const = lambda i: (0, 0)
    whole = lambda a: pl.BlockSpec(a.shape, const)
    halo_blocks = ROW_TILE // HALO
    return pl.pallas_call(
        functools.partial(_ffn_kernel, final_norm=final_norm),
        out_shape=jax.ShapeDtypeStruct((t, D_MODEL), F32),
        grid=(t // ROW_TILE,),
        in_specs=[
            pl.BlockSpec((ROW_TILE, D_MODEL), lambda i: (i, 0)),
            pl.BlockSpec((HALO, D_MODEL), lambda i: (jnp.maximum(i * halo_blocks - 1, 0), 0)),
            whole(g),
            pl.BlockSpec(wup.shape, const, pipeline_mode=pl.Buffered(1)),
            whole(cw), whole(cb),
            pl.BlockSpec(wdn.shape, const, pipeline_mode=pl.Buffered(1)),
            whole(fg),
        ],
        out_specs=pl.BlockSpec((ROW_TILE, D_MODEL), lambda i: (i, 0)),
        scratch_shapes=[pltpu.VMEM((ROW_TILE + HALO, D_MODEL), BF16), pltpu.VMEM((ROW_TILE, D_FF), BF16)],
        compiler_params=_cparams(("parallel",)),
        name="conv_ffn",
    )(x2, x2, g, wup, cw, cb, wdn, fg)


def _nsa_slot_perm():
    cols = []
    for j in range(NSA_HEADS // 2):
        for e in range(2):
            h = _nsa_head(j, e)
            cols += list(range(h * HEAD_DIM, (h + 1) * HEAD_DIM))
    return np.asarray(cols)


def _constants():
    perm = _nsa_slot_perm()
    eg = np.zeros((LANES, 3 * NSA_W), np.float32)
    for newcol, oldcol in enumerate(perm):
        h = oldcol // HEAD_DIM
        for br in range(3):
            eg[SMALL_GC + h * 3 + br, br * NSA_W + newcol] = 1.0
    ovt = np.zeros((N_SEL, LANES), np.float32)
    n_cmp = (SEQ - NSA_CMP_BLOCK) // NSA_CMP_STRIDE + 1
    for c in range(n_cmp):
        cs, ce = c * NSA_CMP_STRIDE, c * NSA_CMP_STRIDE + NSA_CMP_BLOCK - 1
        for jb in range(N_SEL):
            if cs < jb * NSA_SEL_BLOCK + NSA_SEL_BLOCK and ce >= jb * NSA_SEL_BLOCK:
                ovt[jb, c] = 1.0
    pos = np.arange(SEQ)
    kx = np.zeros((SEQ, LANES), np.float32)
    for g in range(NSA_KV_GROUPS):
        kx[pos, g * N_SEL + pos // NSA_SEL_BLOCK] = 1.0
    kx[:, XL_POS:XL_POS + 3] = (256 * (pos // 256))[:, None]
    kx[:, XL_POS + 3:XL_POS + 6] = (pos % 256)[:, None]
    qx = np.zeros((8, LANES), np.float32)
    for hs in range(NSA_HEADS):
        s = np.float32(NSA_SLOPES[_nsa_head(hs // 2, hs % 2)])
        hi = np.float32(np.asarray(s, dtype=BF16))
        mid = np.float32(np.asarray(s - hi, dtype=BF16))
        lo = np.float32(np.asarray(s - hi - mid, dtype=BF16))
        qx[hs, XL_POS:XL_POS + 6] = [hi, mid, lo, hi, mid, lo]
    return (jnp.asarray(eg, BF16), jnp.asarray(ovt, BF16), jnp.asarray(kx, BF16), jnp.asarray(qx, F32))


def _layer_params(l, w_in, b_in, cmp_pe, cmp_k_w1, cmp_k_w2, cmp_v_w1, cmp_v_w2, w_pc):
    o = np.cumsum([0, FOX_W, FOX_W, FOX_W, FOX_HEADS, DIL_W, DIL_W, DIL_W, NSA_W,
                   NSA_KV_W, NSA_KV_W, NSA_KV_W, NSA_KV_W, NSA_KV_W, NSA_KV_W, NSA_HEADS * 3, 3 * D_MODEL])
    (o_qa, o_ka, o_va, o_fa, o_qb, o_kb, o_vb, o_qc, o_kc, o_vc, o_ks, o_vs, o_kw, o_vw, o_gc, o_g, _) = [int(v) for v in o]
    w, b = w_in[l], b_in[l]
    heads = [_nsa_head(j, e) for j in range(NSA_HEADS // 2) for e in range(2)]
    pieces = ([(o_qa, o_qa + 3 * FOX_W), (o_qb, o_qb + 3 * DIL_W)]
              + [(o_qc + h * HEAD_DIM, o_qc + (h + 1) * HEAD_DIM) for h in heads]
              + [(o_ks, o_ks + 4 * NSA_KV_W), (o_kc, o_kc + 2 * NSA_KV_W)])
    pad = LANES - FOX_HEADS - NSA_HEADS * 3
    p = {
        "w_qkv": jnp.concatenate([w[:, s:e] for s, e in pieces], axis=1).astype(BF16),
        "b_qkv": jnp.concatenate([b[s:e] for s, e in pieces])[None, :],
        "w_small": jnp.concatenate([w[:, o_fa:o_fa + FOX_HEADS], w[:, o_gc:o_g],
                                    jnp.zeros((D_MODEL, pad), w.dtype)], axis=1).astype(BF16),
        "b_small": jnp.concatenate([b[o_fa:o_fa + FOX_HEADS], b[o_gc:o_g], jnp.zeros((pad,), b.dtype)])[None, :],
        "w_gate": w[:, o_g:].astype(BF16),
        "b_gate": b[o_g:][None, :],
        "w_pc": jnp.concatenate([w_pc[l][h * HEAD_DIM:(h + 1) * HEAD_DIM] for h in heads], axis=0).astype(BF16),
    }
    pe = cmp_pe[l].reshape(2, NSA_CMP_STRIDE, 1, HEAD_DIM)
    pe = jnp.broadcast_to(pe, (2, NSA_CMP_STRIDE, NSA_KV_GROUPS, HEAD_DIM)).reshape(2, NSA_CMP_STRIDE * LANES)
    p["pe2"] = jnp.concatenate([pe, jnp.zeros((PE_ROWS - 2, NSA_CMP_STRIDE * LANES), pe.dtype)]).astype(BF16)
    for name, w1, w2 in (("k", cmp_k_w1[l], cmp_k_w2[l]), ("v", cmp_v_w1[l], cmp_v_w2[l])):
        for half, tag in enumerate("ab"):
            wh = w1[half * NSA_CMP_STRIDE * HEAD_DIM:(half + 1) * NSA_CMP_STRIDE * HEAD_DIM]
            wh = wh.reshape(NSA_CMP_STRIDE, HEAD_DIM, NSA_CMP_HIDDEN)
            z = jnp.zeros_like(wh)
            bd = jnp.stack([jnp.concatenate([wh, z], 2), jnp.concatenate([z, wh], 2)], axis=1)
            p["w1" + tag + name] = bd.reshape(NSA_CMP_STRIDE * LANES, NSA_KV_GROUPS * NSA_CMP_HIDDEN).astype(BF16)
        z = jnp.zeros_like(w2)
        p["w2" + name] = jnp.concatenate([jnp.concatenate([w2, z], 1), jnp.concatenate([z, w2], 1)], 0).astype(BF16)
    return p


def kernel(x, norm1_g, w_in, b_in, cmp_pe, cmp_k_w1, cmp_k_w2, cmp_v_w1, cmp_v_w2, w_pa, w_pb, w_pc, w_o,
           norm2_g, w_up, conv_w, conv_b, w_down, final_g):
    bsz, seq, d = x.shape
    assert (seq, d) == (SEQ, D_MODEL)
    depth = w_in.shape[0]
    eg, ovt, kx, qx = _constants()
    x2 = x.reshape(bsz * seq, d)
    for l in range(depth):
        p = _layer_params(l, w_in, b_in, cmp_pe, cmp_k_w1, cmp_k_w2, cmp_v_w1, cmp_v_w2, w_pc)
        g1 = norm1_g[l][None, :]
        qkv, small = _inproj(x2, g1, p["w_qkv"], p["b_qkv"], p["w_small"], p["b_small"])
        qkv3 = qkv.reshape(bsz, seq, QKV_W)
        small3 = small.reshape(bsz, seq, LANES)
        ya = _fox(qkv3, small3)
        yb = _dil(qkv3)
        kc2, vc2 = _compress(qkv3, p["pe2"], p["w1ak"], p["w1bk"], p["w2k"], p["w1av"], p["w1bv"], p["w2v"])
        yc = _nsa(qkv3, kc2, vc2, small3, eg, ovt, kx, qx)
        x2 = _merge(x2, g1, ya.reshape(-1, FOX_W), yb.reshape(-1, DIL_W), yc.reshape(-1, NSA_W),
                    p["w_gate"], p["b_gate"], w_pa[l].astype(BF16), w_pb[l].astype(BF16), p["w_pc"],
                    w_o[l].astype(BF16))
        x2 = _ffn(x2, norm2_g[l][None, :], w_up[l].astype(BF16), conv_w[l], conv_b[l][None, :],
                  w_down[l].astype(BF16), final_g[None, :], final_norm=(l == depth - 1))
    return x2.reshape(bsz, seq, d)
```

```python
import functools

import numpy as np
import jax
import jax.numpy as jnp
from jax import lax
from jax.experimental import pallas as pl
from jax.experimental.pallas import tpu as pltpu

D_MODEL = 1024
SEQ = 2048
HEAD_DIM = 64
RMS_EPS = 1e-6
NEG_INF = -1e30
TINY = 1e-30

FOX_HEADS = 4
DIL_WINDOWS = (128, 512, 2048)
DIL_RATES = (1, 4, 16)
DIL_HEADS = 6
NSA_HEADS = 6
NSA_KV_GROUPS = 2
NSA_CMP_BLOCK = 32
NSA_CMP_STRIDE = 16
NSA_CMP_HIDDEN = 256
NSA_SEL_BLOCK = 64
NSA_SEL_TOPK = 8
NSA_WINDOW = 512
FORCE_SCORE = 1e6
ALIBI_HEADS = DIL_HEADS + NSA_HEADS
D_FF = 2816
CONV_WIDTH = 3

FOX_W = FOX_HEADS * HEAD_DIM
DIL_W = DIL_HEADS * HEAD_DIM
NSA_W = NSA_HEADS * HEAD_DIM
NSA_KV_W = NSA_KV_GROUPS * HEAD_DIM
N_SEL = SEQ // NSA_SEL_BLOCK
N_CMP_PAD = SEQ // NSA_CMP_STRIDE

LANES = 128
VMEM_LIMIT_BYTES = 56 * 1024 * 1024

QKV_W = 3072
COL_FOX_Q, COL_FOX_K, COL_FOX_V = 0, 256, 512
COL_DIL_Q, COL_DIL_K, COL_DIL_V = 768, 1152, 1536
COL_NSA_Q = 1920
COL_NSA_KSEL, COL_NSA_VSEL, COL_NSA_KWIN, COL_NSA_VWIN = 2304, 2432, 2560, 2688
COL_NSA_CMP = 2816
SMALL_FA = 0
SMALL_GC = 4

ROW_TILE = 512
FF_CHUNK = 256
HALO = 16

ATT_TQ = 256
ATT_TK = 512
WIN_SPAN = NSA_WINDOW + ATT_TQ
DIL_BAND = DIL_WINDOWS[0] // DIL_RATES[0]
DIL_SPAN = ATT_TQ + DIL_BAND

F32 = jnp.float32
BF16 = jnp.bfloat16


def _alibi_slopes_np(n):
    return np.exp2(-8.0 * np.arange(1, n + 1, dtype=np.float64) / n)


_SLOPES = _alibi_slopes_np(ALIBI_HEADS)
DIL_SLOPES = [float(np.float32(s)) for s in _SLOPES[0::2]]
NSA_SLOPES = [float(np.float32(s)) for s in _SLOPES[1::2]]


def _cparams(sem):
    return pltpu.CompilerParams(dimension_semantics=sem, vmem_limit_bytes=VMEM_LIMIT_BYTES)


def _rms(x, g):
    return x * lax.rsqrt(jnp.mean(x * x, axis=-1, keepdims=True) + RMS_EPS) * g


def _gelu_tanh(x):
    return 0.5 * x * (1.0 + jnp.tanh(np.float32(np.sqrt(2.0 / np.pi)) * (x + 0.044715 * (x * x * x))))


def _dot(a, b):
    return jnp.dot(a, b, preferred_element_type=F32)


def _dot_nt(a, b):
    return lax.dot_general(a, b, (((1,), (1,)), ((), ())), preferred_element_type=F32)


def _split3(x):
    hi = x.astype(BF16)
    r = x - hi.astype(F32)
    mid = r.astype(BF16)
    lo = (r - mid.astype(F32)).astype(BF16)
    return hi, mid, lo


def _inproj_kernel(x_ref, g_ref, w_ref, b_ref, ws_ref, bs_ref, qkv_ref, small_ref):
    h = _rms(x_ref[...], g_ref[...]).astype(BF16)
    for c in range(0, QKV_W, 512):
        qkv_ref[:, c:c + 512] = (_dot(h, w_ref[:, c:c + 512]) + b_ref[:, c:c + 512]).astype(BF16)
    small_ref[...] = _dot(h, ws_ref[...]) + bs_ref[...]


def _inproj(x2, g, w, b, ws, bs):
    t = x2.shape[0]
    const = lambda i: (0, 0)
    return pl.pallas_call(
        _inproj_kernel,
        out_shape=(jax.ShapeDtypeStruct((t, QKV_W), BF16), jax.ShapeDtypeStruct((t, LANES), F32)),
        grid=(t // ROW_TILE,),
        in_specs=[
            pl.BlockSpec((ROW_TILE, D_MODEL), lambda i: (i, 0)),
            pl.BlockSpec((1, D_MODEL), const),
            pl.BlockSpec((D_MODEL, QKV_W), const),
            pl.BlockSpec((1, QKV_W), const),
            pl.BlockSpec((D_MODEL, LANES), const),
            pl.BlockSpec((1, LANES), const),
        ],
        out_specs=(pl.BlockSpec((ROW_TILE, QKV_W), lambda i: (i, 0)),
                   pl.BlockSpec((ROW_TILE, LANES), lambda i: (i, 0))),
        compiler_params=_cparams(("parallel",)),
        name="inproj",
    )(x2, g, w, b, ws, bs)


def _own_and_ones(x, low_half):
    lane = lax.broadcasted_iota(jnp.int32, x.shape, 1)
    own = (lane < HEAD_DIM) if low_half else (lane >= HEAD_DIM)
    return jnp.where(own, x, jnp.ones_like(x))


def _own_and_zeros(x, low_half):
    lane = lax.broadcasted_iota(jnp.int32, x.shape, 1)
    own = (lane < HEAD_DIM) if low_half else (lane >= HEAD_DIM)
    return jnp.where(own, x, jnp.zeros_like(x))


def _softmax_step(chunks, off, m_old, v_tile):
    mx = chunks[0]
    for sc in chunks[1:]:
        mx = jnp.maximum(mx, sc)
    m_new = jnp.maximum(m_old, jnp.max(mx, axis=1, keepdims=True) + off)
    shift = m_new - off
    p = jnp.concatenate([jnp.exp(sc - shift).astype(BF16) for sc in chunks], axis=1)
    return m_new, jnp.exp(m_old - m_new), _dot(p, v_tile)


def _normalise(acc):
    return acc / jnp.maximum(pltpu.roll(acc, HEAD_DIM, 1), TINY)


FOX_XROWS = 16


def _rows_to_lanes(x):
    full = jnp.concatenate([x, jnp.zeros((LANES - FOX_XROWS, SEQ), F32)], axis=0)
    return full.T.astype(BF16)


def _fox_kernel(q_ref, k_ref, v_ref, small_ref, o_ref, qaug_ref, kaug_ref, vaug_ref, qst_ref, m_ref, acc_ref):
    qi = pl.program_id(1)
    tq, tk = ATT_TQ, ATT_TK
    nslot = FOX_HEADS // 2

    @pl.when(qi == 0)
    def _():
        fa = small_ref[...].T[0:8, :]
        ls = jnp.minimum(fa, 0.0) - jnp.log(1.0 + jnp.exp(-jnp.abs(fa)))
        pos = lax.broadcasted_iota(jnp.int32, ls.shape, 1)
        c = ls
        sh = 1
        while sh < SEQ:
            c = c + jnp.where(pos >= sh, pltpu.roll(c, sh, 1), 0.0)
            sh *= 2
        pieces = [p.astype(F32) for p in _split3(c)]
        xrow = lax.broadcasted_iota(jnp.int32, (FOX_XROWS, SEQ), 0)
        for j in range(nslot):
            kx = jnp.where((xrow >= 6) & (xrow < 12), 1.0, 0.0)
            for e in range(2):
                h = 2 * j + e
                qx = jnp.where((xrow >= 3 * e) & (xrow < 3 * e + 3), -1.0, 0.0)
                for i, piece in enumerate(pieces):
                    kx = jnp.where(xrow == 3 * e + i, piece[h:h + 1, :], kx)
                    qx = jnp.where(xrow == 6 + 3 * e + i, piece[h:h + 1, :], qx)
                qaug_ref[h] = _rows_to_lanes(qx)
            kaug_ref[j] = _rows_to_lanes(kx)
            v2 = v_ref[:, j * LANES:(j + 1) * LANES]
            vaug_ref[2 * j] = _own_and_ones(v2, True)
            vaug_ref[2 * j + 1] = _own_and_ones(v2, False)

    row0 = pl.multiple_of(qi * tq, tq)
    scale = jnp.asarray(HEAD_DIM ** -0.5, BF16)
    lane = lax.broadcasted_iota(jnp.int32, (tq, LANES), 1)
    rloc = lax.broadcasted_iota(jnp.int32, (tq, LANES), 0)
    for j in range(nslot):
        q2 = q_ref[:, j * LANES:(j + 1) * LANES] * scale
        for e in range(2):
            qst_ref[j, e * tq:(e + 1) * tq, 0:LANES] = _own_and_zeros(q2, e == 0)
            qst_ref[j, e * tq:(e + 1) * tq, LANES:2 * LANES] = qaug_ref[2 * j + e, pl.ds(row0, tq), :]
    m_ref[...] = jnp.full(m_ref.shape, NEG_INF, F32)
    acc_ref[...] = jnp.zeros(acc_ref.shape, F32)

    def tile(kb, diag):
        start = pl.multiple_of(kb * tk, tk)
        for j in range(nslot):
            keys = jnp.concatenate([k_ref[pl.ds(start, tk), j * LANES:(j + 1) * LANES],
                                    kaug_ref[j, pl.ds(start, tk), :]], axis=1)
            s_slot = _dot_nt(qst_ref[j], keys)
            for e in range(2):
                h = 2 * j + e
                chunks = []
                for ck in range(tk // LANES):
                    sc = s_slot[e * tq:(e + 1) * tq, ck * LANES:(ck + 1) * LANES]
                    if diag:
                        sc = jnp.where((lane - rloc) <= (row0 - start - ck * LANES), sc, NEG_INF)
                    chunks.append(sc)
                m_new, alpha, pv = _softmax_step(chunks, 0.0, m_ref[h], vaug_ref[h, pl.ds(start, tk), :])
                m_ref[h] = m_new
                acc_ref[h] = alpha * acc_ref[h] + pv

    nfull = (qi * tq) // tk

    def body(kb, carry):
        tile(kb, False)
        return carry

    lax.fori_loop(0, nfull, body, 0)
    tile(nfull, True)

    for j in range(nslot):
        o_ref[:, j * LANES:(j + 1) * LANES] = jnp.where(
            lane < HEAD_DIM, _normalise(acc_ref[2 * j]), _normalise(acc_ref[2 * j + 1])).astype(BF16)


def _fox(qkv3, small3):
    b = qkv3.shape[0]
    return pl.pallas_call(
        _fox_kernel,
        out_shape=jax.ShapeDtypeStruct((b, SEQ, FOX_W), BF16),
        grid=(b, SEQ // ATT_TQ),
        in_specs=[
            pl.BlockSpec((None, ATT_TQ, FOX_W), lambda bi, qi: (bi, qi, COL_FOX_Q // FOX_W)),
            pl.BlockSpec((None, SEQ, FOX_W), lambda bi, qi: (bi, 0, COL_FOX_K // FOX_W)),
            pl.BlockSpec((None, SEQ, FOX_W), lambda bi, qi: (bi, 0, COL_FOX_V // FOX_W)),
            pl.BlockSpec((None, SEQ, LANES), lambda bi, qi: (bi, 0, 0)),
        ],
        out_specs=pl.BlockSpec((None, ATT_TQ, FOX_W), lambda bi, qi: (bi, qi, 0)),
        scratch_shapes=[
            pltpu.VMEM((FOX_HEADS, SEQ, LANES), BF16),
            pltpu.VMEM((FOX_HEADS // 2, SEQ, LANES), BF16),
            pltpu.VMEM((FOX_HEADS, SEQ, LANES), BF16),
            pltpu.VMEM((FOX_HEADS // 2, 2 * ATT_TQ, 2 * LANES), BF16),
            pltpu.VMEM((FOX_HEADS, ATT_TQ, LANES), F32),
            pltpu.VMEM((FOX_HEADS, ATT_TQ, LANES), F32),
        ],
        compiler_params=_cparams(("parallel", "arbitrary")),
        name="fox_attention",
    )(qkv3, qkv3, qkv3, small3)


def _dil_bias(g, e, delta):
    tq = ATT_TQ
    r = DIL_RATES[g]
    n = SEQ // r
    span = DIL_SPAN if n > tq else tq
    row = lax.broadcasted_iota(jnp.int32, (tq, span), 0)
    col = lax.broadcasted_iota(jnp.int32, (tq, span), 1)
    d = (row - col) + delta
    ok = (d >= 0) & (d <= DIL_BAND)
    if n <= tq:
        ok = ok & ((row // n) == (col // n))
    return jnp.where(ok, (-DIL_SLOPES[2 * g + e] * r) * d.astype(F32), NEG_INF)


def _dil_kernel(q_ref, k_ref, v_ref, o_ref, stage_ref, qd_ref, kd_ref, vaug_ref, out32_ref, lse_ref,
                bias_ref, bias_last_ref):
    tq = ATT_TQ
    ngroup = len(DIL_RATES)
    scale = jnp.asarray(HEAD_DIM ** -0.5, BF16)

    @pl.when(pl.program_id(0) == 0)
    def _():
        for g in range(ngroup - 1):
            for e in range(2):
                bias_ref[g, e, 0] = _dil_bias(g, e, 0)
                bias_ref[g, e, 1] = _dil_bias(g, e, DIL_BAND)
        for e in range(2):
            bias_last_ref[e] = _dil_bias(ngroup - 1, e, 0)

    for g, r in enumerate(DIL_RATES):
        n = SEQ // r
        cols = slice(g * LANES, (g + 1) * LANES)
        for which, src in enumerate((q_ref, k_ref, v_ref)):
            x = src[:, cols]
            if which == 0:
                x = x * scale
            if r > 1:
                stage_ref[...] = x.astype(F32)
                x = jnp.concatenate([stage_ref[pl.ds(rho, n, stride=r), :] for rho in range(r)], axis=0).astype(BF16)
            if which == 0:
                qd_ref[g, 0] = _own_and_zeros(x, True)
                qd_ref[g, 1] = _own_and_zeros(x, False)
            elif which == 1:
                kd_ref[g] = x
            else:
                vaug_ref[g, 0] = _own_and_ones(x, True)
                vaug_ref[g, 1] = _own_and_ones(x, False)

    low = lax.broadcasted_iota(jnp.int32, (tq, LANES), 1) < HEAD_DIM
    assert all(SEQ // r > tq for r in DIL_RATES[:-1]) and SEQ // DIL_RATES[-1] <= tq

    def tile(i):
        u0 = pl.multiple_of(i * tq, tq)
        for g, r in enumerate(DIL_RATES):
            n = SEQ // r
            span = DIL_SPAN if n > tq else tq
            if n > tq:
                cstart = (u0 // n) * n
                kstart = pl.multiple_of(cstart + jnp.maximum(u0 - cstart - DIL_BAND, 0), LANES)
            else:
                kstart = u0
            later = jnp.minimum(u0 - kstart, 1)
            k2 = kd_ref[g, pl.ds(kstart, span), :]
            halves, lses = [], []
            for e in range(2):
                s = _dot_nt(qd_ref[g, e, pl.ds(u0, tq), :], k2)
                chunks = []
                for ck in range(span // LANES):
                    cols = slice(ck * LANES, (ck + 1) * LANES)
                    bias = bias_ref[g, e, later, :, cols] if n > tq else bias_last_ref[e, :, cols]
                    chunks.append(s[:, cols] + bias)
                mx = chunks[0]
                for sc in chunks[1:]:
                    mx = jnp.maximum(mx, sc)
                m = jnp.broadcast_to(jnp.max(mx, axis=1, keepdims=True), (tq, LANES))
                p = jnp.concatenate([jnp.exp(sc - m).astype(BF16) for sc in chunks], axis=1)
                pv = _dot(p, vaug_ref[g, e, pl.ds(kstart, span), :])
                l = jnp.maximum(pltpu.roll(pv, HEAD_DIM, 1), TINY)
                halves.append(pv / l)
                lses.append(m + jnp.log(l))
            o_slot = jnp.where(low, halves[0], halves[1])
            lse_slot = jnp.where(low, lses[0], lses[1])
            if r == 1:
                out32_ref[g, pl.ds(u0, tq), :] = o_slot
                lse_ref[g, pl.ds(u0, tq), :] = lse_slot
            elif n > tq:
                rho = u0 // n
                t0 = (u0 - rho * n) * r + rho
                out32_ref[g, pl.ds(t0, tq, stride=r), :] = o_slot
                lse_ref[g, pl.ds(t0, tq, stride=r), :] = lse_slot
            else:
                for cc in range(tq // n):
                    rho = u0 // n + cc
                    out32_ref[g, pl.ds(rho, n, stride=r), :] = o_slot[cc * n:(cc + 1) * n, :]
                    lse_ref[g, pl.ds(rho, n, stride=r), :] = lse_slot[cc * n:(cc + 1) * n, :]

    def two_tiles(i, carry):
        tile(2 * i)
        tile(2 * i + 1)
        return carry

    lax.fori_loop(0, SEQ // (2 * tq), two_tiles, 0)

    top = jnp.maximum(jnp.maximum(lse_ref[0], lse_ref[1]), lse_ref[2])
    ws = [jnp.exp(lse_ref[g] - top) for g in range(ngroup)]
    den = ws[0] + ws[1] + ws[2]
    for g in range(ngroup):
        o_ref[:, g * LANES:(g + 1) * LANES] = (out32_ref[g] * (ws[g] / den)).astype(BF16)


def _dil(qkv3):
    b = qkv3.shape[0]
    ngroup = len(DIL_RATES)
    blk = lambda col: pl.BlockSpec((None, SEQ, DIL_W), lambda bi: (bi, 0, col // DIL_W))
    return pl.pallas_call(
        _dil_kernel,
        out_shape=jax.ShapeDtypeStruct((b, SEQ, DIL_W), BF16),
        grid=(b,),
        in_specs=[blk(COL_DIL_Q), blk(COL_DIL_K), blk(COL_DIL_V)],
        out_specs=pl.BlockSpec((None, SEQ, DIL_W), lambda bi: (bi, 0, 0)),
        scratch_shapes=[
            pltpu.VMEM((SEQ, LANES), F32),
            pltpu.VMEM((ngroup, 2, SEQ, LANES), BF16),
            pltpu.VMEM((ngroup, SEQ, LANES), BF16),
            pltpu.VMEM((ngroup, 2, SEQ, LANES), BF16),
            pltpu.VMEM((ngroup, SEQ, LANES), F32),
            pltpu.VMEM((ngroup, SEQ, LANES), F32),
            pltpu.VMEM((ngroup - 1, 2, 2, ATT_TQ, DIL_SPAN), F32),
            pltpu.VMEM((2, ATT_TQ, ATT_TQ), F32),
        ],
        compiler_params=_cparams(("arbitrary",)),
        name="dilated_attention",
    )(qkv3, qkv3, qkv3)


PE_ROWS = 16


def _cmp_kernel(x_ref, pe_ref, w1ak_ref, w1bk_ref, w2k_ref, w1av_ref, w1bv_ref, w2v_ref, kc_ref, vc_ref,
                stage_ref, xp_ref):
    xp_ref[N_CMP_PAD:, :] = pe_ref[...]
    for kind, (w1a_ref, w1b_ref, w2_ref, out_ref) in enumerate(
            ((w1ak_ref, w1bk_ref, w2k_ref, kc_ref), (w1av_ref, w1bv_ref, w2v_ref, vc_ref))):
        stage_ref[...] = x_ref[:, kind * LANES:(kind + 1) * LANES].astype(F32)
        for m in range(NSA_CMP_STRIDE):
            xp_ref[0:N_CMP_PAD, m * LANES:(m + 1) * LANES] = stage_ref[
                pl.ds(m, N_CMP_PAD, stride=NSA_CMP_STRIDE), :].astype(BF16)
        xp = xp_ref[...]
        a = _dot(xp, w1a_ref[...])
        b = _dot(xp, w1b_ref[...])
        cpe = a[N_CMP_PAD:N_CMP_PAD + 1, :] + b[N_CMP_PAD + 1:N_CMP_PAD + 2, :]
        bnext = pltpu.roll(b[0:N_CMP_PAD, :], N_CMP_PAD - 1, 0)
        h1 = _gelu_tanh(a[0:N_CMP_PAD, :] + bnext + cpe)
        out_ref[...] = _dot(h1.astype(BF16), w2_ref[...]).astype(BF16)


def _compress(qkv3, pe2, w1ak, w1bk, w2k, w1av, w1bv, w2v):
    b = qkv3.shape[0]
    whole = lambda a: pl.BlockSpec(a.shape, lambda bi: (0, 0))
    cmp_w = 2 * NSA_KV_W
    return pl.pallas_call(
        _cmp_kernel,
        out_shape=(jax.ShapeDtypeStruct((b, N_CMP_PAD, LANES), BF16),) * 2,
        grid=(b,),
        in_specs=[pl.BlockSpec((None, SEQ, cmp_w), lambda bi: (bi, 0, COL_NSA_CMP // cmp_w)),
                  whole(pe2), whole(w1ak), whole(w1bk), whole(w2k), whole(w1av), whole(w1bv), whole(w2v)],
        out_specs=(pl.BlockSpec((None, N_CMP_PAD, LANES), lambda bi: (bi, 0, 0)),) * 2,
        scratch_shapes=[pltpu.VMEM((SEQ, LANES), F32),
                        pltpu.VMEM((N_CMP_PAD + PE_ROWS, NSA_CMP_STRIDE * LANES), BF16)],
        compiler_params=_cparams(("parallel",)),
        name="nsa_compress",
    )(qkv3, pe2, w1ak, w1bk, w2k, w1av, w1bv, w2v)


def _nsa_head(j, e):
    return j + (NSA_HEADS // NSA_KV_GROUPS) * e


XL_POS = 2 * N_SEL


def _nsa_kernel(q_ref, ksl_ref, vsl_ref, kwn_ref, vwn_ref, kc_ref, vc_ref, small_ref,
                eg_ref, ovt_ref, kx_ref, qx_ref, o_ref,
                vaug_ref, qst_ref, score_ref, selp_ref, m_ref, acc_ref):
    qi = pl.program_id(1)
    tq, tk = ATT_TQ, ATT_TK
    nhs = NSA_HEADS
    row0 = pl.multiple_of(qi * tq, tq)
    lane = lax.broadcasted_iota(jnp.int32, (tq, LANES), 1)
    rloc = lax.broadcasted_iota(jnp.int32, (tq, LANES), 0)
    low = lane < HEAD_DIM
    slope_of = [NSA_SLOPES[_nsa_head(hs // 2, hs % 2)] for hs in range(nhs)]

    @pl.when(qi == 0)
    def _():
        for i, src in enumerate((vsl_ref, vwn_ref)):
            vaug_ref[2 * i] = _own_and_ones(src[...], True)
            vaug_ref[2 * i + 1] = _own_and_ones(src[...], False)

    scale = jnp.asarray(HEAD_DIM ** -0.5, BF16)
    for j in range(nhs // 2):
        q2 = q_ref[:, j * LANES:(j + 1) * LANES] * scale
        for e in range(2):
            hs = 2 * j + e
            qst_ref[hs * tq:(hs + 1) * tq, 0:LANES] = _own_and_zeros(q2, e == 0)
            qst_ref[hs * tq:(hs + 1) * tq, LANES:2 * LANES] = jnp.broadcast_to(
                qx_ref[hs:hs + 1, :], (tq, LANES)).astype(BF16)

    sg = jax.nn.sigmoid(small_ref[...])
    ghi = sg.astype(BF16)
    glo = (sg - ghi.astype(F32)).astype(BF16)
    gexp = _dot(ghi, eg_ref[...]) + _dot(glo, eg_ref[...])

    tpos = row0 + rloc
    tposf = tpos.astype(F32)
    row_off = [-slope_of[hs] * tposf for hs in range(nhs)]

    wstart = pl.multiple_of(jnp.maximum(row0 - NSA_WINDOW, 0), tq)
    delta_w = row0 - wstart
    keys_w = jnp.concatenate([kwn_ref[pl.ds(wstart, WIN_SPAN), :], kx_ref[pl.ds(wstart, WIN_SPAN), :]], axis=1)
    s_w = _dot_nt(qst_ref[...], keys_w)
    in_window = []
    for ck in range(WIN_SPAN // LANES):
        d = (rloc - lane) + (delta_w - ck * LANES)
        in_window.append(jnp.where((d >= 0) & (d < NSA_WINDOW), 0.0, NEG_INF))
    o_win = []
    for hs in range(nhs):
        chunks = [s_w[hs * tq:(hs + 1) * tq, ck * LANES:(ck + 1) * LANES] + in_window[ck]
                  for ck in range(WIN_SPAN // LANES)]
        _, _, pv = _softmax_step(chunks, row_off[hs], jnp.full((tq, LANES), NEG_INF, F32),
                                 vaug_ref[2 + hs % 2, pl.ds(wstart, WIN_SPAN), :])
        o_win.append(pv)

    dist_c = tpos - (lane * NSA_CMP_STRIDE + NSA_CMP_BLOCK - 1)
    valid_c = dist_c >= 0
    dist_cf = dist_c.astype(F32)
    s_c = _dot_nt(qst_ref[:, 0:LANES], kc_ref[...])
    psum = [jnp.zeros((tq, LANES), F32) for _ in range(NSA_KV_GROUPS)]
    o_cmp = []
    for hs in range(nhs):
        s = s_c[hs * tq:(hs + 1) * tq, :] - slope_of[hs] * dist_cf
        s = jnp.where(valid_c, s, NEG_INF)
        m = jnp.max(s, axis=1, keepdims=True)
        p = jnp.where(valid_c, jnp.exp(s - m), 0.0)
        p = p / jnp.maximum(jnp.sum(p, axis=1, keepdims=True), TINY)
        psum[hs % 2] = psum[hs % 2] + p
        o_cmp.append(_dot(p.astype(BF16), vc_ref[...]))

    jrow = lax.broadcasted_iota(jnp.int32, (N_SEL, tq), 0)
    qblk = (row0 + lax.broadcasted_iota(jnp.int32, (N_SEL, tq), 1)) >> 6
    causal = jrow <= qblk
    forced = (jrow == 0) | (jrow == qblk) | (jrow == qblk - 1)
    selp_ref[NSA_KV_GROUPS * N_SEL:, :] = jnp.zeros((LANES - NSA_KV_GROUPS * N_SEL, tq), F32)
    for g in range(NSA_KV_GROUPS):
        imp = sum(_dot_nt(ovt_ref[...], part) for part in _split3(psum[g]))
        score = jnp.where(causal & forced, FORCE_SCORE, jnp.where(causal, imp, -1.0))
        score_ref[g] = score
        rank = jnp.zeros((N_SEL, tq), jnp.int32)
        for jp in range(N_SEL):
            other = score_ref[g, jp:jp + 1, :]
            ahead = (other > score) | ((other == score) & (jp < jrow))
            rank = rank + ahead.astype(jnp.int32)
        sel = (rank < NSA_SEL_TOPK) & causal
        selp_ref[g * N_SEL:(g + 1) * N_SEL, :] = jnp.where(sel, 0.0, NEG_INF)
    sel_t = selp_ref[...].T
    for hs in range(nhs):
        g = hs % 2
        mine = (lane >= g * N_SEL) & (lane < (g + 1) * N_SEL)
        extra = jnp.where(mine, sel_t, 0.0) + qx_ref[hs:hs + 1, :]
        qst_ref[hs * tq:(hs + 1) * tq, LANES:2 * LANES] = extra.astype(BF16)

    m_ref[...] = jnp.full(m_ref.shape, NEG_INF, F32)
    acc_ref[...] = jnp.zeros(acc_ref.shape, F32)

    def sel_tile(kb, diag):
        start = pl.multiple_of(kb * tk, tk)
        keys = jnp.concatenate([ksl_ref[pl.ds(start, tk), :], kx_ref[pl.ds(start, tk), :]], axis=1)
        s_all = _dot_nt(qst_ref[...], keys)
        for hs in range(nhs):
            chunks = []
            for ck in range(tk // LANES):
                sc = s_all[hs * tq:(hs + 1) * tq, ck * LANES:(ck + 1) * LANES]
                if diag:
                    sc = jnp.where((lane - rloc) <= (row0 - start - ck * LANES), sc, NEG_INF)
                chunks.append(sc)
            m_new, alpha, pv = _softmax_step(chunks, row_off[hs], m_ref[hs], vaug_ref[hs % 2, pl.ds(start, tk), :])
            m_ref[hs] = m_new
            acc_ref[hs] = alpha * acc_ref[hs] + pv

    nfull = (qi * tq) // tk

    def body(i, carry):
        sel_tile(nfull - 1 - i, False)
        return carry

    sel_tile(nfull, True)
    lax.fori_loop(0, nfull, body, 0)

    for j in range(nhs // 2):
        oc = jnp.where(low, o_cmp[2 * j], o_cmp[2 * j + 1])
        osl = jnp.where(low, _normalise(acc_ref[2 * j]), _normalise(acc_ref[2 * j + 1]))
        ow = jnp.where(low, _normalise(o_win[2 * j]), _normalise(o_win[2 * j + 1]))
        g_c = gexp[:, 0 * NSA_W + j * LANES:0 * NSA_W + (j + 1) * LANES]
        g_s = gexp[:, 1 * NSA_W + j * LANES:1 * NSA_W + (j + 1) * LANES]
        g_w = gexp[:, 2 * NSA_W + j * LANES:2 * NSA_W + (j + 1) * LANES]
        o_ref[:, j * LANES:(j + 1) * LANES] = (g_c * oc + g_s * osl + g_w * ow).astype(BF16)


def _nsa(qkv3, kc2, vc2, small3, eg, ovt, kx, qx):
    b = qkv3.shape[0]
    full = lambda col: pl.BlockSpec((None, SEQ, LANES), lambda bi, qi: (bi, 0, col // LANES))
    c2 = lambda bi, qi: (0, 0)
    return pl.pallas_call(
        _nsa_kernel,
        out_shape=jax.ShapeDtypeStruct((b, SEQ, NSA_W), BF16),
        grid=(b, SEQ // ATT_TQ),
        in_specs=[
            pl.BlockSpec((None, ATT_TQ, NSA_W), lambda bi, qi: (bi, qi, COL_NSA_Q // NSA_W)),
            full(COL_NSA_KSEL), full(COL_NSA_VSEL), full(COL_NSA_KWIN), full(COL_NSA_VWIN),
            pl.BlockSpec((None, N_CMP_PAD, LANES), lambda bi, qi: (bi, 0, 0)),
            pl.BlockSpec((None, N_CMP_PAD, LANES), lambda bi, qi: (bi, 0, 0)),
            pl.BlockSpec((None, ATT_TQ, LANES), lambda bi, qi: (bi, qi, 0)),
            pl.BlockSpec((LANES, 3 * NSA_W), c2),
            pl.BlockSpec((N_SEL, LANES), c2),
            pl.BlockSpec((SEQ, LANES), c2),
            pl.BlockSpec((8, LANES), c2),
        ],
        out_specs=pl.BlockSpec((None, ATT_TQ, NSA_W), lambda bi, qi: (bi, qi, 0)),
        scratch_shapes=[
            pltpu.VMEM((4, SEQ, LANES), BF16),
            pltpu.VMEM((NSA_HEADS * ATT_TQ, 2 * LANES), BF16),
            pltpu.VMEM((NSA_KV_GROUPS, N_SEL, ATT_TQ), F32),
            pltpu.VMEM((LANES, ATT_TQ), F32),
            pltpu.VMEM((NSA_HEADS, ATT_TQ, LANES), F32),
            pltpu.VMEM((NSA_HEADS, ATT_TQ, LANES), F32),
        ],
        compiler_params=_cparams(("parallel", "arbitrary")),
        name="nsa_attention",
    )(qkv3, qkv3, qkv3, qkv3, qkv3, kc2, vc2, small3, eg, ovt, kx, qx)


def _merge_kernel(x_ref, g_ref, ya_ref, yb_ref, yc_ref, wg_ref, bg_ref, wpa_ref, wpb_ref, wpc_ref, wo_ref, o_ref):
    x = x_ref[...]
    h = _rms(x, g_ref[...]).astype(BF16)
    merged = None
    for i, (y_ref, wp_ref) in enumerate(((ya_ref, wpa_ref), (yb_ref, wpb_ref), (yc_ref, wpc_ref))):
        cols = slice(i * D_MODEL, (i + 1) * D_MODEL)
        gate = jax.nn.sigmoid(_dot(h, wg_ref[:, cols]) + bg_ref[:, cols])
        term = gate * _dot(y_ref[...], wp_ref[...])
        merged = term if merged is None else merged + term
    o_ref[...] = x + _dot(merged.astype(BF16), wo_ref[...])


def _merge(x2, g, ya, yb, yc, wg, bg, wpa, wpb, wpc, wo):
    t = x2.shape[0]
    const = lambda i: (0, 0)
    rows = lambda w: pl.BlockSpec((ROW_TILE, w), lambda i: (i, 0))
    whole = lambda a: pl.BlockSpec(a.shape, const)
    return pl.pallas_call(
        _merge_kernel,
        out_shape=jax.ShapeDtypeStruct((t, D_MODEL), F32),
        grid=(t // ROW_TILE,),
        in_specs=[rows(D_MODEL), whole(g), rows(FOX_W), rows(DIL_W), rows(NSA_W),
                  whole(wg), whole(bg), whole(wpa), whole(wpb), whole(wpc), whole(wo)],
        out_specs=rows(D_MODEL),
        compiler_params=_cparams(("parallel",)),
        name="merge_oproj",
    )(x2, g, ya, yb, yc, wg, bg, wpa, wpb, wpc, wo)


def _ffn_kernel(x_ref, xh_ref, g_ref, wup_ref, cw_ref, cb_ref, wdn_ref, fg_ref, o_ref, h_ref, act_ref, *, final_norm):
    i = pl.program_id(0)
    x = x_ref[...]
    g = g_ref[...]
    h_ref[0:HALO, :] = _rms(xh_ref[...], g).astype(BF16)
    h_ref[HALO:, :] = _rms(x, g).astype(BF16)
    row = lax.broadcasted_iota(jnp.int32, (ROW_TILE + HALO, FF_CHUNK), 0)
    seq_start = (i % (SEQ // ROW_TILE)) == 0
    keep = jnp.logical_not((row < HALO) & seq_start)
    for c in range(0, D_FF, FF_CHUNK):
        a = jnp.where(keep, _dot(h_ref[...], wup_ref[:, c:c + FF_CHUNK]), 0.0)
        conv = (a * cw_ref[2:3, c:c + FF_CHUNK]
                + pltpu.roll(a, 1, 0) * cw_ref[1:2, c:c + FF_CHUNK]
                + pltpu.roll(a, 2, 0) * cw_ref[0:1, c:c + FF_CHUNK]
                + cb_ref[:, c:c + FF_CHUNK])
        gate = _dot(h_ref[HALO:, :], wup_ref[:, D_FF + c:D_FF + c + FF_CHUNK])
        act_ref[:, c:c + FF_CHUNK] = (_gelu_tanh(conv[HALO:, :]) * gate).astype(BF16)
    y = x + _dot(act_ref[...], wdn_ref[...])
    if final_norm:
        y = _rms(y, fg_ref[...])
    o_ref[...] = y


def _ffn(x2, g, wup, cw, cb, wdn, fg, final_norm):
    t = x2.shape[0]
    const = lambda i: (0, 0)
    whole = lambda a: pl.BlockSpec(a.shape, const)
    halo_blocks = ROW_TILE // HALO
    return pl.pallas_call(
        functools.partial(_ffn_kernel, final_norm=final_norm),
        out_shape=jax.ShapeDtypeStruct((t, D_MODEL), F32),
        grid=(t // ROW_TILE,),
        in_specs=[
            pl.BlockSpec((ROW_TILE, D_MODEL), lambda i: (i, 0)),
            pl.BlockSpec((HALO, D_MODEL), lambda i: (jnp.maximum(i * halo_blocks - 1, 0), 0)),
            whole(g),
            pl.BlockSpec(wup.shape, const, pipeline_mode=pl.Buffered(1)),
            whole(cw), whole(cb),
            pl.BlockSpec(wdn.shape, const, pipeline_mode=pl.Buffered(1)),
            whole(fg),
        ],
        out_specs=pl.BlockSpec((ROW_TILE, D_MODEL), lambda i: (i, 0)),
        scratch_shapes=[pltpu.VMEM((ROW_TILE + HALO, D_MODEL), BF16), pltpu.VMEM((ROW_TILE, D_FF), BF16)],
        compiler_params=_cparams(("parallel",)),
        name="conv_ffn",
    )(x2, x2, g, wup, cw, cb, wdn, fg)


def _nsa_slot_perm():
    cols = []
    for j in range(NSA_HEADS // 2):
        for e in range(2):
            h = _nsa_head(j, e)
            cols += list(range(h * HEAD_DIM, (h + 1) * HEAD_DIM))
    return np.asarray(cols)


def _constants():
    perm = _nsa_slot_perm()
    eg = np.zeros((LANES, 3 * NSA_W), np.float32)
    for newcol, oldcol in enumerate(perm):
        h = oldcol // HEAD_DIM
        for br in range(3):
            eg[SMALL_GC + h * 3 + br, br * NSA_W + newcol] = 1.0
    ovt = np.zeros((N_SEL, LANES), np.float32)
    n_cmp = (SEQ - NSA_CMP_BLOCK) // NSA_CMP_STRIDE + 1
    for c in range(n_cmp):
        cs, ce = c * NSA_CMP_STRIDE, c * NSA_CMP_STRIDE + NSA_CMP_BLOCK - 1
        for jb in range(N_SEL):
            if cs < jb * NSA_SEL_BLOCK + NSA_SEL_BLOCK and ce >= jb * NSA_SEL_BLOCK:
                ovt[jb, c] = 1.0
    pos = np.arange(SEQ)
    kx = np.zeros((SEQ, LANES), np.float32)
    for g in range(NSA_KV_GROUPS):
        kx[pos, g * N_SEL + pos // NSA_SEL_BLOCK] = 1.0
    kx[:, XL_POS:XL_POS + 3] = (256 * (pos // 256))[:, None]
    kx[:, XL_POS + 3:XL_POS + 6] = (pos % 256)[:, None]
    qx = np.zeros((8, LANES), np.float32)
    for hs in range(NSA_HEADS):
        s = np.float32(NSA_SLOPES[_nsa_head(hs // 2, hs % 2)])
        hi = np.float32(np.asarray(s, dtype=BF16))
        mid = np.float32(np.asarray(s - hi, dtype=BF16))
        lo = np.float32(np.asarray(s - hi - mid, dtype=BF16))
        qx[hs, XL_POS:XL_POS + 6] = [hi, mid, lo, hi, mid, lo]
    return (jnp.asarray(eg, BF16), jnp.asarray(ovt, BF16), jnp.asarray(kx, BF16), jnp.asarray(qx, F32))


def _layer_params(l, w_in, b_in, cmp_pe, cmp_k_w1, cmp_k_w2, cmp_v_w1, cmp_v_w2, w_pc):
    o = np.cumsum([0, FOX_W, FOX_W, FOX_W, FOX_HEADS, DIL_W, DIL_W, DIL_W, NSA_W,
                   NSA_KV_W, NSA_KV_W, NSA_KV_W, NSA_KV_W, NSA_KV_W, NSA_KV_W, NSA_HEADS * 3, 3 * D_MODEL])
    (o_qa, o_ka, o_va, o_fa, o_qb, o_kb, o_vb, o_qc, o_kc, o_vc, o_ks, o_vs, o_kw, o_vw, o_gc, o_g, _) = [int(v) for v in o]
    w, b = w_in[l], b_in[l]
    heads = [_nsa_head(j, e) for j in range(NSA_HEADS // 2) for e in range(2)]
    pieces = ([(o_qa, o_qa + 3 * FOX_W), (o_qb, o_qb + 3 * DIL_W)]
              + [(o_qc + h * HEAD_DIM, o_qc + (h + 1) * HEAD_DIM) for h in heads]
              + [(o_ks, o_ks + 4 * NSA_KV_W), (o_kc, o_kc + 2 * NSA_KV_W)])
    order = np.concatenate([np.arange(s, e) for s, e in pieces])
    small_cols = np.concatenate([np.arange(o_fa, o_fa + FOX_HEADS), np.arange(o_gc, o_g)])
    pad = LANES - small_cols.size
    p = {
        "w_qkv": w[:, order].astype(BF16),
        "b_qkv": b[order][None, :],
        "w_small": jnp.pad(w[:, small_cols], ((0, 0), (0, pad))).astype(BF16),
        "b_small": jnp.pad(b[small_cols], (0, pad))[None, :],
        "w_gate": w[:, o_g:].astype(BF16),
        "b_gate": b[o_g:][None, :],
        "w_pc": jnp.concatenate([w_pc[l][h * HEAD_DIM:(h + 1) * HEAD_DIM] for h in heads], axis=0).astype(BF16),
    }
    pe = cmp_pe[l].reshape(2, NSA_CMP_STRIDE, 1, HEAD_DIM)
    pe = jnp.broadcast_to(pe, (2, NSA_CMP_STRIDE, NSA_KV_GROUPS, HEAD_DIM)).reshape(2, NSA_CMP_STRIDE * LANES)
    p["pe2"] = jnp.concatenate([pe, jnp.zeros((PE_ROWS - 2, NSA_CMP_STRIDE * LANES), pe.dtype)]).astype(BF16)
    for name, w1, w2 in (("k", cmp_k_w1[l], cmp_k_w2[l]), ("v", cmp_v_w1[l], cmp_v_w2[l])):
        for half, tag in enumerate("ab"):
            wh = w1[half * NSA_CMP_STRIDE * HEAD_DIM:(half + 1) * NSA_CMP_STRIDE * HEAD_DIM]
            wh = wh.reshape(NSA_CMP_STRIDE, HEAD_DIM, NSA_CMP_HIDDEN)
            z = jnp.zeros_like(wh)
            bd = jnp.stack([jnp.concatenate([wh, z], 2), jnp.concatenate([z, wh], 2)], axis=1)
            p["w1" + tag + name] = bd.reshape(NSA_CMP_STRIDE * LANES, NSA_KV_GROUPS * NSA_CMP_HIDDEN).astype(BF16)
        z = jnp.zeros_like(w2)
        p["w2" + name] = jnp.concatenate([jnp.concatenate([w2, z], 1), jnp.concatenate([z, w2], 1)], 0).astype(BF16)
    return p


def kernel(x, norm1_g, w_in, b_in, cmp_pe, cmp_k_w1, cmp_k_w2, cmp_v_w1, cmp_v_w2, w_pa, w_pb, w_pc, w_o,
           norm2_g, w_up, conv_w, conv_b, w_down, final_g):
    bsz, seq, d = x.shape
    assert (seq, d) == (SEQ, D_MODEL)
    depth = w_in.shape[0]
    eg, ovt, kx, qx = _constants()
    x2 = x.reshape(bsz * seq, d)
    for l in range(depth):
        p = _layer_params(l, w_in, b_in, cmp_pe, cmp_k_w1, cmp_k_w2, cmp_v_w1, cmp_v_w2, w_pc)
        g1 = norm1_g[l][None, :]
        qkv, small = _inproj(x2, g1, p["w_qkv"], p["b_qkv"], p["w_small"], p["b_small"])
        qkv3 = qkv.reshape(bsz, seq, QKV_W)
        small3 = small.reshape(bsz, seq, LANES)
        ya = _fox(qkv3, small3)
        yb = _dil(qkv3)
        kc2, vc2 = _compress(qkv3, p["pe2"], p["w1ak"], p["w1bk"], p["w2k"], p["w1av"], p["w1bv"], p["w2v"])
        yc = _nsa(qkv3, kc2, vc2, small3, eg, ovt, kx, qx)
        x2 = _merge(x2, g1, ya.reshape(-1, FOX_W), yb.reshape(-1, DIL_W), yc.reshape(-1, NSA_W),
                    p["w_gate"], p["b_gate"], w_pa[l].astype(BF16), w_pb[l].astype(BF16), p["w_pc"],
                    w_o[l].astype(BF16))
        x2 = _ffn(x2, norm2_g[l][None, :], w_up[l].astype(BF16), conv_w[l], conv_b[l][None, :],
                  w_down[l].astype(BF16), final_g[None, :], final_norm=(l == depth - 1))
    return x2.reshape(bsz, seq, d)
```

```python
import functools

import numpy as np
import jax
import jax.numpy as jnp
from jax import lax
from jax.experimental import pallas as pl
from jax.experimental.pallas import tpu as pltpu

D_MODEL = 1024
SEQ = 2048
HEAD_DIM = 64
RMS_EPS = 1e-6
NEG_INF = -1e30
TINY = 1e-30

FOX_HEADS = 4
DIL_WINDOWS = (128, 512, 2048)
DIL_RATES = (1, 4, 16)
DIL_HEADS = 6
NSA_HEADS = 6
NSA_KV_GROUPS = 2
NSA_CMP_BLOCK = 32
NSA_CMP_STRIDE = 16
NSA_CMP_HIDDEN = 256
NSA_SEL_BLOCK = 64
NSA_SEL_TOPK = 8
NSA_WINDOW = 512
FORCE_SCORE = 1e6
ALIBI_HEADS = DIL_HEADS + NSA_HEADS
D_FF = 2816
CONV_WIDTH = 3

FOX_W = FOX_HEADS * HEAD_DIM
DIL_W = DIL_HEADS * HEAD_DIM
NSA_W = NSA_HEADS * HEAD_DIM
NSA_KV_W = NSA_KV_GROUPS * HEAD_DIM
N_SEL = SEQ // NSA_SEL_BLOCK
N_CMP_PAD = SEQ // NSA_CMP_STRIDE

LANES = 128
VMEM_LIMIT_BYTES = 56 * 1024 * 1024

QKV_W = 3072
COL_FOX_Q, COL_FOX_K, COL_FOX_V = 0, 256, 512
COL_DIL_Q, COL_DIL_K, COL_DIL_V = 768, 1152, 1536
COL_NSA_Q = 1920
COL_NSA_KSEL, COL_NSA_VSEL, COL_NSA_KWIN, COL_NSA_VWIN = 2304, 2432, 2560, 2688
COL_NSA_CMP = 2816
SMALL_FA = 0
SMALL_GC = 4

ROW_TILE = 512
FF_CHUNK = 256
HALO = 16

ATT_TQ = 256
ATT_TK = 512
WIN_SPAN = NSA_WINDOW + ATT_TQ
DIL_BAND = DIL_WINDOWS[0] // DIL_RATES[0]
DIL_SPAN = ATT_TQ + DIL_BAND

F32 = jnp.float32
BF16 = jnp.bfloat16


def _alibi_slopes_np(n):
    return np.exp2(-8.0 * np.arange(1, n + 1, dtype=np.float64) / n)


_SLOPES = _alibi_slopes_np(ALIBI_HEADS)
DIL_SLOPES = [float(np.float32(s)) for s in _SLOPES[0::2]]
NSA_SLOPES = [float(np.float32(s)) for s in _SLOPES[1::2]]


def _cparams(sem):
    return pltpu.CompilerParams(dimension_semantics=sem, vmem_limit_bytes=VMEM_LIMIT_BYTES)


def _rms(x, g):
    return x * lax.rsqrt(jnp.mean(x * x, axis=-1, keepdims=True) + RMS_EPS) * g


def _gelu_tanh(x):
    return 0.5 * x * (1.0 + jnp.tanh(np.float32(np.sqrt(2.0 / np.pi)) * (x + 0.044715 * (x * x * x))))


def _dot(a, b):
    return jnp.dot(a, b, preferred_element_type=F32)


def _dot_nt(a, b):
    return lax.dot_general(a, b, (((1,), (1,)), ((), ())), preferred_element_type=F32)


def _split3(x):
    hi = x.astype(BF16)
    r = x - hi.astype(F32)
    mid = r.astype(BF16)
    lo = (r - mid.astype(F32)).astype(BF16)
    return hi, mid, lo


def _inproj_kernel(x_ref, g_ref, w_ref, b_ref, ws_ref, bs_ref, qkv_ref, small_ref):
    h = _rms(x_ref[...], g_ref[...]).astype(BF16)
    for c in range(0, QKV_W, 512):
        qkv_ref[:, c:c + 512] = (_dot(h, w_ref[:, c:c + 512]) + b_ref[:, c:c + 512]).astype(BF16)
    small_ref[...] = _dot(h, ws_ref[...]) + bs_ref[...]


def _inproj(x2, g, w, b, ws, bs):
    t = x2.shape[0]
    const = lambda i: (0, 0)
    return pl.pallas_call(
        _inproj_kernel,
        out_shape=(jax.ShapeDtypeStruct((t, QKV_W), BF16), jax.ShapeDtypeStruct((t, LANES), F32)),
        grid=(t // ROW_TILE,),
        in_specs=[
            pl.BlockSpec((ROW_TILE, D_MODEL), lambda i: (i, 0)),
            pl.BlockSpec((1, D_MODEL), const),
            pl.BlockSpec((D_MODEL, QKV_W), const),
            pl.BlockSpec((1, QKV_W), const),
            pl.BlockSpec((D_MODEL, LANES), const),
            pl.BlockSpec((1, LANES), const),
        ],
        out_specs=(pl.BlockSpec((ROW_TILE, QKV_W), lambda i: (i, 0)),
                   pl.BlockSpec((ROW_TILE, LANES), lambda i: (i, 0))),
        compiler_params=_cparams(("parallel",)),
        name="inproj",
    )(x2, g, w, b, ws, bs)


def _own_and_ones(x, low_half):
    lane = lax.broadcasted_iota(jnp.int32, x.shape, 1)
    own = (lane < HEAD_DIM) if low_half else (lane >= HEAD_DIM)
    return jnp.where(own, x, jnp.ones_like(x))


def _own_and_zeros(x, low_half):
    lane = lax.broadcasted_iota(jnp.int32, x.shape, 1)
    own = (lane < HEAD_DIM) if low_half else (lane >= HEAD_DIM)
    return jnp.where(own, x, jnp.zeros_like(x))


def _softmax_step(chunks, off, m_old, v_tile):
    mx = chunks[0]
    for sc in chunks[1:]:
        mx = jnp.maximum(mx, sc)
    m_new = jnp.maximum(m_old, jnp.max(mx, axis=1, keepdims=True) + off)
    shift = m_new - off
    p = jnp.concatenate([jnp.exp(sc - shift).astype(BF16) for sc in chunks], axis=1)
    return m_new, jnp.exp(m_old - m_new), _dot(p, v_tile)


def _normalise(acc):
    return acc / jnp.maximum(pltpu.roll(acc, HEAD_DIM, 1), TINY)


FOX_XROWS = 16


def _rows_to_lanes(x):
    full = jnp.concatenate([x, jnp.zeros((LANES - FOX_XROWS, SEQ), F32)], axis=0)
    return full.T.astype(BF16)


def _fox_kernel(q_ref, k_ref, v_ref, small_ref, o_ref, qaug_ref, kaug_ref, vaug_ref, qst_ref, m_ref, acc_ref):
    qi = pl.program_id(1)
    tq, tk = ATT_TQ, ATT_TK
    nslot = FOX_HEADS // 2

    @pl.when(qi == 0)
    def _():
        fa = small_ref[...].T[0:8, :]
        ls = jnp.minimum(fa, 0.0) - jnp.log(1.0 + jnp.exp(-jnp.abs(fa)))
        pos = lax.broadcasted_iota(jnp.int32, ls.shape, 1)
        c = ls
        sh = 1
        while sh < SEQ:
            c = c + jnp.where(pos >= sh, pltpu.roll(c, sh, 1), 0.0)
            sh *= 2
        pieces = [p.astype(F32) for p in _split3(c)]
        xrow = lax.broadcasted_iota(jnp.int32, (FOX_XROWS, SEQ), 0)
        for j in range(nslot):
            kx = jnp.where((xrow >= 6) & (xrow < 12), 1.0, 0.0)
            for e in range(2):
                h = 2 * j + e
                qx = jnp.where((xrow >= 3 * e) & (xrow < 3 * e + 3), -1.0, 0.0)
                for i, piece in enumerate(pieces):
                    kx = jnp.where(xrow == 3 * e + i, piece[h:h + 1, :], kx)
                    qx = jnp.where(xrow == 6 + 3 * e + i, piece[h:h + 1, :], qx)
                qaug_ref[h] = _rows_to_lanes(qx)
            kaug_ref[j] = _rows_to_lanes(kx)
            v2 = v_ref[:, j * LANES:(j + 1) * LANES]
            vaug_ref[2 * j] = _own_and_ones(v2, True)
            vaug_ref[2 * j + 1] = _own_and_ones(v2, False)

    row0 = pl.multiple_of(qi * tq, tq)
    scale = jnp.asarray(HEAD_DIM ** -0.5, BF16)
    lane = lax.broadcasted_iota(jnp.int32, (tq, LANES), 1)
    rloc = lax.broadcasted_iota(jnp.int32, (tq, LANES), 0)
    for j in range(nslot):
        q2 = q_ref[:, j * LANES:(j + 1) * LANES] * scale
        for e in range(2):
            qst_ref[j, e * tq:(e + 1) * tq, 0:LANES] = _own_and_zeros(q2, e == 0)
            qst_ref[j, e * tq:(e + 1) * tq, LANES:2 * LANES] = qaug_ref[2 * j + e, pl.ds(row0, tq), :]
    m_ref[...] = jnp.full(m_ref.shape, NEG_INF, F32)
    acc_ref[...] = jnp.zeros(acc_ref.shape, F32)

    def tile(kb, diag):
        start = pl.multiple_of(kb * tk, tk)
        for j in range(nslot):
            keys = jnp.concatenate([k_ref[pl.ds(start, tk), j * LANES:(j + 1) * LANES],
                                    kaug_ref[j, pl.ds(start, tk), :]], axis=1)
            s_slot = _dot_nt(qst_ref[j], keys)
            for e in range(2):
                h = 2 * j + e
                chunks = []
                for ck in range(tk // LANES):
                    sc = s_slot[e * tq:(e + 1) * tq, ck * LANES:(ck + 1) * LANES]
                    if diag:
                        sc = jnp.where((lane - rloc) <= (row0 - start - ck * LANES), sc, NEG_INF)
                    chunks.append(sc)
                m_new, alpha, pv = _softmax_step(chunks, 0.0, m_ref[h], vaug_ref[h, pl.ds(start, tk), :])
                m_ref[h] = m_new
                acc_ref[h] = alpha * acc_ref[h] + pv

    nfull = (qi * tq) // tk

    def body(kb, carry):
        tile(kb, False)
        return carry

    lax.fori_loop(0, nfull, body, 0)
    tile(nfull, True)

    for j in range(nslot):
        o_ref[:, j * LANES:(j + 1) * LANES] = jnp.where(
            lane < HEAD_DIM, _normalise(acc_ref[2 * j]), _normalise(acc_ref[2 * j + 1])).astype(BF16)


def _fox(qkv3, small3):
    b = qkv3.shape[0]
    return pl.pallas_call(
        _fox_kernel,
        out_shape=jax.ShapeDtypeStruct((b, SEQ, FOX_W), BF16),
        grid=(b, SEQ // ATT_TQ),
        in_specs=[
            pl.BlockSpec((None, ATT_TQ, FOX_W), lambda bi, qi: (bi, qi, COL_FOX_Q // FOX_W)),
            pl.BlockSpec((None, SEQ, FOX_W), lambda bi, qi: (bi, 0, COL_FOX_K // FOX_W)),
            pl.BlockSpec((None, SEQ, FOX_W), lambda bi, qi: (bi, 0, COL_FOX_V // FOX_W)),
            pl.BlockSpec((None, SEQ, LANES), lambda bi, qi: (bi, 0, 0)),
        ],
        out_specs=pl.BlockSpec((None, ATT_TQ, FOX_W), lambda bi, qi: (bi, qi, 0)),
        scratch_shapes=[
            pltpu.VMEM((FOX_HEADS, SEQ, LANES), BF16),
            pltpu.VMEM((FOX_HEADS // 2, SEQ, LANES), BF16),
            pltpu.VMEM((FOX_HEADS, SEQ, LANES), BF16),
            pltpu.VMEM((FOX_HEADS // 2, 2 * ATT_TQ, 2 * LANES), BF16),
            pltpu.VMEM((FOX_HEADS, ATT_TQ, LANES), F32),
            pltpu.VMEM((FOX_HEADS, ATT_TQ, LANES), F32),
        ],
        compiler_params=_cparams(("parallel", "arbitrary")),
        name="fox_attention",
    )(qkv3, qkv3, qkv3, small3)


def _dil_bias(g, e, delta):
    tq = ATT_TQ
    r = DIL_RATES[g]
    n = SEQ // r
    span = DIL_SPAN if n > tq else tq
    row = lax.broadcasted_iota(jnp.int32, (tq, span), 0)
    col = lax.broadcasted_iota(jnp.int32, (tq, span), 1)
    d = (row - col) + delta
    ok = (d >= 0) & (d <= DIL_BAND)
    if n <= tq:
        ok = ok & ((row // n) == (col // n))
    return jnp.where(ok, (-DIL_SLOPES[2 * g + e] * r) * d.astype(F32), NEG_INF)


def _dil_kernel(q_ref, k_ref, v_ref, o_ref, stage_ref, qd_ref, kd_ref, vaug_ref, out32_ref, lse_ref,
                bias_ref, bias_last_ref):
    tq = ATT_TQ
    ngroup = len(DIL_RATES)
    scale = jnp.asarray(HEAD_DIM ** -0.5, BF16)

    @pl.when(pl.program_id(0) == 0)
    def _():
        for g in range(ngroup - 1):
            for e in range(2):
                bias_ref[g, e, 0] = _dil_bias(g, e, 0)
                bias_ref[g, e, 1] = _dil_bias(g, e, DIL_BAND)
        for e in range(2):
            bias_last_ref[e] = _dil_bias(ngroup - 1, e, 0)

    for g, r in enumerate(DIL_RATES):
        n = SEQ // r
        cols = slice(g * LANES, (g + 1) * LANES)
        for which, src in enumerate((q_ref, k_ref, v_ref)):
            x = src[:, cols]
            if which == 0:
                x = x * scale
            if r > 1:
                stage_ref[...] = x.astype(F32)
                x = jnp.concatenate([stage_ref[pl.ds(rho, n, stride=r), :] for rho in range(r)], axis=0).astype(BF16)
            if which == 0:
                qd_ref[g, 0] = _own_and_zeros(x, True)
                qd_ref[g, 1] = _own_and_zeros(x, False)
            elif which == 1:
                kd_ref[g] = x
            else:
                vaug_ref[g, 0] = _own_and_ones(x, True)
                vaug_ref[g, 1] = _own_and_ones(x, False)

    low = lax.broadcasted_iota(jnp.int32, (tq, LANES), 1) < HEAD_DIM
    assert all(SEQ // r > tq for r in DIL_RATES[:-1]) and SEQ // DIL_RATES[-1] <= tq

    def tile(i):
        u0 = pl.multiple_of(i * tq, tq)
        for g, r in enumerate(DIL_RATES):
            n = SEQ // r
            span = DIL_SPAN if n > tq else tq
            if n > tq:
                cstart = (u0 // n) * n
                kstart = pl.multiple_of(cstart + jnp.maximum(u0 - cstart - DIL_BAND, 0), LANES)
            else:
                kstart = u0
            later = jnp.minimum(u0 - kstart, 1)
            k2 = kd_ref[g, pl.ds(kstart, span), :]
            halves, lses = [], []
            for e in range(2):
                s = _dot_nt(qd_ref[g, e, pl.ds(u0, tq), :], k2)
                chunks = []
                for ck in range(span // LANES):
                    cols = slice(ck * LANES, (ck + 1) * LANES)
                    bias = bias_ref[g, e, later, :, cols] if n > tq else bias_last_ref[e, :, cols]
                    chunks.append(s[:, cols] + bias)
                mx = chunks[0]
                for sc in chunks[1:]:
                    mx = jnp.maximum(mx, sc)
                m = jnp.broadcast_to(jnp.max(mx, axis=1, keepdims=True), (tq, LANES))
                p = jnp.concatenate([jnp.exp(sc - m).astype(BF16) for sc in chunks], axis=1)
                pv = _dot(p, vaug_ref[g, e, pl.ds(kstart, span), :])
                l = jnp.maximum(pltpu.roll(pv, HEAD_DIM, 1), TINY)
                halves.append(pv / l)
                lses.append(m + jnp.log(l))
            o_slot = jnp.where(low, halves[0], halves[1])
            lse_slot = jnp.where(low, lses[0], lses[1])
            if r == 1:
                out32_ref[g, pl.ds(u0, tq), :] = o_slot
                lse_ref[g, pl.ds(u0, tq), :] = lse_slot
            elif n > tq:
                rho = u0 // n
                t0 = (u0 - rho * n) * r + rho
                out32_ref[g, pl.ds(t0, tq, stride=r), :] = o_slot
                lse_ref[g, pl.ds(t0, tq, stride=r), :] = lse_slot
            else:
                for cc in range(tq // n):
                    rho = u0 // n + cc
                    out32_ref[g, pl.ds(rho, n, stride=r), :] = o_slot[cc * n:(cc + 1) * n, :]
                    lse_ref[g, pl.ds(rho, n, stride=r), :] = lse_slot[cc * n:(cc + 1) * n, :]

    def two_tiles(i, carry):
        tile(2 * i)
        tile(2 * i + 1)
        return carry

    lax.fori_loop(0, SEQ // (2 * tq), two_tiles, 0)

    top = jnp.maximum(jnp.maximum(lse_ref[0], lse_ref[1]), lse_ref[2])
    ws = [jnp.exp(lse_ref[g] - top) for g in range(ngroup)]
    den = ws[0] + ws[1] + ws[2]
    for g in range(ngroup):
        o_ref[:, g * LANES:(g + 1) * LANES] = (out32_ref[g] * (ws[g] / den)).astype(BF16)


def _dil(qkv3):
    b = qkv3.shape[0]
    ngroup = len(DIL_RATES)
    blk = lambda col: pl.BlockSpec((None, SEQ, DIL_W), lambda bi: (bi, 0, col // DIL_W))
    return pl.pallas_call(
        _dil_kernel,
        out_shape=jax.ShapeDtypeStruct((b, SEQ, DIL_W), BF16),
        grid=(b,),
        in_specs=[blk(COL_DIL_Q), blk(COL_DIL_K), blk(COL_DIL_V)],
        out_specs=pl.BlockSpec((None, SEQ, DIL_W), lambda bi: (bi, 0, 0)),
        scratch_shapes=[
            pltpu.VMEM((SEQ, LANES), F32),
            pltpu.VMEM((ngroup, 2, SEQ, LANES), BF16),
            pltpu.VMEM((ngroup, SEQ, LANES), BF16),
            pltpu.VMEM((ngroup, 2, SEQ, LANES), BF16),
            pltpu.VMEM((ngroup, SEQ, LANES), F32),
            pltpu.VMEM((ngroup, SEQ, LANES), F32),
            pltpu.VMEM((ngroup - 1, 2, 2, ATT_TQ, DIL_SPAN), F32),
            pltpu.VMEM((2, ATT_TQ, ATT_TQ), F32),
        ],
        compiler_params=_cparams(("arbitrary",)),
        name="dilated_attention",
    )(qkv3, qkv3, qkv3)


PE_ROWS = 16


def _cmp_kernel(x_ref, pe_ref, w1ak_ref, w1bk_ref, w2k_ref, w1av_ref, w1bv_ref, w2v_ref, kc_ref, vc_ref,
                stage_ref, xp_ref):
    xp_ref[N_CMP_PAD:, :] = pe_ref[...]
    for kind, (w1a_ref, w1b_ref, w2_ref, out_ref) in enumerate(
            ((w1ak_ref, w1bk_ref, w2k_ref, kc_ref), (w1av_ref, w1bv_ref, w2v_ref, vc_ref))):
        stage_ref[...] = x_ref[:, kind * LANES:(kind + 1) * LANES].astype(F32)
        for m in range(NSA_CMP_STRIDE):
            xp_ref[0:N_CMP_PAD, m * LANES:(m + 1) * LANES] = stage_ref[
                pl.ds(m, N_CMP_PAD, stride=NSA_CMP_STRIDE), :].astype(BF16)
        xp = xp_ref[...]
        a = _dot(xp, w1a_ref[...])
        b = _dot(xp, w1b_ref[...])
        cpe = a[N_CMP_PAD:N_CMP_PAD + 1, :] + b[N_CMP_PAD + 1:N_CMP_PAD + 2, :]
        bnext = pltpu.roll(b[0:N_CMP_PAD, :], N_CMP_PAD - 1, 0)
        h1 = _gelu_tanh(a[0:N_CMP_PAD, :] + bnext + cpe)
        out_ref[...] = _dot(h1.astype(BF16), w2_ref[...]).astype(BF16)


def _compress(qkv3, pe2, w1ak, w1bk, w2k, w1av, w1bv, w2v):
    b = qkv3.shape[0]
    whole = lambda a: pl.BlockSpec(a.shape, lambda bi: (0, 0))
    cmp_w = 2 * NSA_KV_W
    return pl.pallas_call(
        _cmp_kernel,
        out_shape=(jax.ShapeDtypeStruct((b, N_CMP_PAD, LANES), BF16),) * 2,
        grid=(b,),
        in_specs=[pl.BlockSpec((None, SEQ, cmp_w), lambda bi: (bi, 0, COL_NSA_CMP // cmp_w)),
                  whole(pe2), whole(w1ak), whole(w1bk), whole(w2k), whole(w1av), whole(w1bv), whole(w2v)],
        out_specs=(pl.BlockSpec((None, N_CMP_PAD, LANES), lambda bi: (bi, 0, 0)),) * 2,
        scratch_shapes=[pltpu.VMEM((SEQ, LANES), F32),
                        pltpu.VMEM((N_CMP_PAD + PE_ROWS, NSA_CMP_STRIDE * LANES), BF16)],
        compiler_params=_cparams(("parallel",)),
        name="nsa_compress",
    )(qkv3, pe2, w1ak, w1bk, w2k, w1av, w1bv, w2v)


def _nsa_head(j, e):
    return j + (NSA_HEADS // NSA_KV_GROUPS) * e


XL_POS = 2 * N_SEL


def _nsa_kernel(q_ref, ksl_ref, vsl_ref, kwn_ref, vwn_ref, kc_ref, vc_ref, small_ref,
                eg_ref, ovt_ref, kx_ref, qx_ref, o_ref,
                vaug_ref, qst_ref, score_ref, selp_ref, m_ref, acc_ref):
    qi = pl.program_id(1)
    tq, tk = ATT_TQ, ATT_TK
    nhs = NSA_HEADS
    row0 = pl.multiple_of(qi * tq, tq)
    lane = lax.broadcasted_iota(jnp.int32, (tq, LANES), 1)
    rloc = lax.broadcasted_iota(jnp.int32, (tq, LANES), 0)
    low = lane < HEAD_DIM
    slope_of = [NSA_SLOPES[_nsa_head(hs // 2, hs % 2)] for hs in range(nhs)]

    @pl.when(qi == 0)
    def _():
        for i, src in enumerate((vsl_ref, vwn_ref)):
            vaug_ref[2 * i] = _own_and_ones(src[...], True)
            vaug_ref[2 * i + 1] = _own_and_ones(src[...], False)

    scale = jnp.asarray(HEAD_DIM ** -0.5, BF16)
    for j in range(nhs // 2):
        q2 = q_ref[:, j * LANES:(j + 1) * LANES] * scale
        for e in range(2):
            hs = 2 * j + e
            qst_ref[hs * tq:(hs + 1) * tq, 0:LANES] = _own_and_zeros(q2, e == 0)
            qst_ref[hs * tq:(hs + 1) * tq, LANES:2 * LANES] = jnp.broadcast_to(
                qx_ref[hs:hs + 1, :], (tq, LANES)).astype(BF16)

    sg = jax.nn.sigmoid(small_ref[...])
    ghi = sg.astype(BF16)
    glo = (sg - ghi.astype(F32)).astype(BF16)
    gexp = _dot(ghi, eg_ref[...]) + _dot(glo, eg_ref[...])

    tpos = row0 + rloc
    tposf = tpos.astype(F32)
    row_off = [-slope_of[hs] * tposf for hs in range(nhs)]

    wstart = pl.multiple_of(jnp.maximum(row0 - NSA_WINDOW, 0), tq)
    delta_w = row0 - wstart
    keys_w = jnp.concatenate([kwn_ref[pl.ds(wstart, WIN_SPAN), :], kx_ref[pl.ds(wstart, WIN_SPAN), :]], axis=1)
    s_w = _dot_nt(qst_ref[...], keys_w)
    in_window = []
    for ck in range(WIN_SPAN // LANES):
        d = (rloc - lane) + (delta_w - ck * LANES)
        in_window.append(jnp.where((d >= 0) & (d < NSA_WINDOW), 0.0, NEG_INF))
    o_win = []
    for hs in range(nhs):
        chunks = [s_w[hs * tq:(hs + 1) * tq, ck * LANES:(ck + 1) * LANES] + in_window[ck]
                  for ck in range(WIN_SPAN // LANES)]
        _, _, pv = _softmax_step(chunks, row_off[hs], jnp.full((tq, LANES), NEG_INF, F32),
                                 vaug_ref[2 + hs % 2, pl.ds(wstart, WIN_SPAN), :])
        o_win.append(pv)

    dist_c = tpos - (lane * NSA_CMP_STRIDE + NSA_CMP_BLOCK - 1)
    valid_c = dist_c >= 0
    dist_cf = dist_c.astype(F32)
    s_c = _dot_nt(qst_ref[:, 0:LANES], kc_ref[...])
    psum = [jnp.zeros((tq, LANES), F32) for _ in range(NSA_KV_GROUPS)]
    o_cmp = []
    for hs in range(nhs):
        s = s_c[hs * tq:(hs + 1) * tq, :] - slope_of[hs] * dist_cf
        s = jnp.where(valid_c, s, NEG_INF)
        m = jnp.max(s, axis=1, keepdims=True)
        p = jnp.where(valid_c, jnp.exp(s - m), 0.0)
        p = p / jnp.maximum(jnp.sum(p, axis=1, keepdims=True), TINY)
        psum[hs % 2] = psum[hs % 2] + p
        o_cmp.append(_dot(p.astype(BF16), vc_ref[...]))

    jrow = lax.broadcasted_iota(jnp.int32, (N_SEL, tq), 0)
    qblk = (row0 + lax.broadcasted_iota(jnp.int32, (N_SEL, tq), 1)) >> 6
    causal = jrow <= qblk
    forced = (jrow == 0) | (jrow == qblk) | (jrow == qblk - 1)
    selp_ref[NSA_KV_GROUPS * N_SEL:, :] = jnp.zeros((LANES - NSA_KV_GROUPS * N_SEL, tq), F32)
    for g in range(NSA_KV_GROUPS):
        imp = sum(_dot_nt(ovt_ref[...], part) for part in _split3(psum[g]))
        score = jnp.where(causal & forced, FORCE_SCORE, jnp.where(causal, imp, -1.0))
        score_ref[g] = score
        rank = jnp.zeros((N_SEL, tq), jnp.int32)
        for jp in range(N_SEL):
            other = score_ref[g, jp:jp + 1, :]
            ahead = (other > score) | ((other == score) & (jp < jrow))
            rank = rank + ahead.astype(jnp.int32)
        sel = (rank < NSA_SEL_TOPK) & causal
        selp_ref[g * N_SEL:(g + 1) * N_SEL, :] = jnp.where(sel, 0.0, NEG_INF)
    sel_t = selp_ref[...].T
    for hs in range(nhs):
        g = hs % 2
        mine = (lane >= g * N_SEL) & (lane < (g + 1) * N_SEL)
        extra = jnp.where(mine, sel_t, 0.0) + qx_ref[hs:hs + 1, :]
        qst_ref[hs * tq:(hs + 1) * tq, LANES:2 * LANES] = extra.astype(BF16)

    m_ref[...] = jnp.full(m_ref.shape, NEG_INF, F32)
    acc_ref[...] = jnp.zeros(acc_ref.shape, F32)

    def sel_tile(kb, diag):
        start = pl.multiple_of(kb * tk, tk)
        keys = jnp.concatenate([ksl_ref[pl.ds(start, tk), :], kx_ref[pl.ds(start, tk), :]], axis=1)
        s_all = _dot_nt(qst_ref[...], keys)
        for hs in range(nhs):
            chunks = []
            for ck in range(tk // LANES):
                sc = s_all[hs * tq:(hs + 1) * tq, ck * LANES:(ck + 1) * LANES]
                if diag:
                    sc = jnp.where((lane - rloc) <= (row0 - start - ck * LANES), sc, NEG_INF)
                chunks.append(sc)
            m_new, alpha, pv = _softmax_step(chunks, row_off[hs], m_ref[hs], vaug_ref[hs % 2, pl.ds(start, tk), :])
            m_ref[hs] = m_new
            acc_ref[hs] = alpha * acc_ref[hs] + pv

    nfull = (qi * tq) // tk

    def body(i, carry):
        sel_tile(nfull - 1 - i, False)
        return carry

    sel_tile(nfull, True)
    lax.fori_loop(0, nfull, body, 0)

    for j in range(nhs // 2):
        oc = jnp.where(low, o_cmp[2 * j], o_cmp[2 * j + 1])
        osl = jnp.where(low, _normalise(acc_ref[2 * j]), _normalise(acc_ref[2 * j + 1]))
        ow = jnp.where(low, _normalise(o_win[2 * j]), _normalise(o_win[2 * j + 1]))
        g_c = gexp[:, 0 * NSA_W + j * LANES:0 * NSA_W + (j + 1) * LANES]
        g_s = gexp[:, 1 * NSA_W + j * LANES:1 * NSA_W + (j + 1) * LANES]
        g_w = gexp[:, 2 * NSA_W + j * LANES:2 * NSA_W + (j + 1) * LANES]
        o_ref[:, j * LANES:(j + 1) * LANES] = (g_c * oc + g_s * osl + g_w * ow).astype(BF16)


def _nsa(qkv3, kc2, vc2, small3, eg, ovt, kx, qx):
    b = qkv3.shape[0]
    full = lambda col: pl.BlockSpec((None, SEQ, LANES), lambda bi, qi: (bi, 0, col // LANES))
    c2 = lambda bi, qi: (0, 0)
    return pl.pallas_call(
        _nsa_kernel,
        out_shape=jax.ShapeDtypeStruct((b, SEQ, NSA_W), BF16),
        grid=(b, SEQ // ATT_TQ),
        in_specs=[
            pl.BlockSpec((None, ATT_TQ, NSA_W), lambda bi, qi: (bi, qi, COL_NSA_Q // NSA_W)),
            full(COL_NSA_KSEL), full(COL_NSA_VSEL), full(COL_NSA_KWIN), full(COL_NSA_VWIN),
            pl.BlockSpec((None, N_CMP_PAD, LANES), lambda bi, qi: (bi, 0, 0)),
            pl.BlockSpec((None, N_CMP_PAD, LANES), lambda bi, qi: (bi, 0, 0)),
            pl.BlockSpec((None, ATT_TQ, LANES), lambda bi, qi: (bi, qi, 0)),
            pl.BlockSpec((LANES, 3 * NSA_W), c2),
            pl.BlockSpec((N_SEL, LANES), c2),
            pl.BlockSpec((SEQ, LANES), c2),
            pl.BlockSpec((8, LANES), c2),
        ],
        out_specs=pl.BlockSpec((None, ATT_TQ, NSA_W), lambda bi, qi: (bi, qi, 0)),
        scratch_shapes=[
            pltpu.VMEM((4, SEQ, LANES), BF16),
            pltpu.VMEM((NSA_HEADS * ATT_TQ, 2 * LANES), BF16),
            pltpu.VMEM((NSA_KV_GROUPS, N_SEL, ATT_TQ), F32),
            pltpu.VMEM((LANES, ATT_TQ), F32),
            pltpu.VMEM((NSA_HEADS, ATT_TQ, LANES), F32),
            pltpu.VMEM((NSA_HEADS, ATT_TQ, LANES), F32),
        ],
        compiler_params=_cparams(("parallel", "arbitrary")),
        name="nsa_attention",
    )(qkv3, qkv3, qkv3, qkv3, qkv3, kc2, vc2, small3, eg, ovt, kx, qx)


def _merge_kernel(x_ref, g_ref, ya_ref, yb_ref, yc_ref, wg_ref, bg_ref, wpa_ref, wpb_ref, wpc_ref, wo_ref, o_ref):
    x = x_ref[...]
    h = _rms(x, g_ref[...]).astype(BF16)
    merged = None
    for i, (y_ref, wp_ref) in enumerate(((ya_ref, wpa_ref), (yb_ref, wpb_ref), (yc_ref, wpc_ref))):
        cols = slice(i * D_MODEL, (i + 1) * D_MODEL)
        gate = jax.nn.sigmoid(_dot(h, wg_ref[:, cols]) + bg_ref[:, cols])
        term = gate * _dot(y_ref[...], wp_ref[...])
        merged = term if merged is None else merged + term
    o_ref[...] = x + _dot(merged.astype(BF16), wo_ref[...])


def _merge(x2, g, ya, yb, yc, wg, bg, wpa, wpb, wpc, wo):
    t = x2.shape[0]
    const = lambda i: (0, 0)
    rows = lambda w: pl.BlockSpec((ROW_TILE, w), lambda i: (i, 0))
    whole = lambda a: pl.BlockSpec(a.shape, const)
    return pl.pallas_call(
        _merge_kernel,
        out_shape=jax.ShapeDtypeStruct((t, D_MODEL), F32),
        grid=(t // ROW_TILE,),
        in_specs=[rows(D_MODEL), whole(g), rows(FOX_W), rows(DIL_W), rows(NSA_W),
                  whole(wg), whole(bg), whole(wpa), whole(wpb), whole(wpc), whole(wo)],
        out_specs=rows(D_MODEL),
        compiler_params=_cparams(("parallel",)),
        name="merge_oproj",
    )(x2, g, ya, yb, yc, wg, bg, wpa, wpb, wpc, wo)


def _ffn_kernel(x_ref, xh_ref, g_ref, wup_ref, cw_ref, cb_ref, wdn_ref, fg_ref, o_ref, h_ref, act_ref, *, final_norm):
    i = pl.program_id(0)
    x = x_ref[...]
    g = g_ref[...]
    h_ref[0:HALO, :] = _rms(xh_ref[...], g).astype(BF16)
    h_ref[HALO:, :] = _rms(x, g).astype(BF16)
    row = lax.broadcasted_iota(jnp.int32, (ROW_TILE + HALO, FF_CHUNK), 0)
    seq_start = (i % (SEQ // ROW_TILE)) == 0
    keep = jnp.logical_not((row < HALO) & seq_start)
    for c in range(0, D_FF, FF_CHUNK):
        a = jnp.where(keep, _dot(h_ref[...], wup_ref[:, c:c + FF_CHUNK]), 0.0)
        conv = (a * cw_ref[2:3, c:c + FF_CHUNK]
                + pltpu.roll(a, 1, 0) * cw_ref[1:2, c:c + FF_CHUNK]
                + pltpu.roll(a, 2, 0) * cw_ref[0:1, c:c + FF_CHUNK]
                + cb_ref[:, c:c + FF_CHUNK])
        gate = _dot(h_ref[HALO:, :], wup_ref[:, D_FF + c:D_FF + c + FF_CHUNK])
        act_ref[:, c:c + FF_CHUNK] = (_gelu_tanh(conv[HALO:, :]) * gate).astype(BF16)
    y = x + _dot(act_ref[...], wdn_ref[...])
    if final_norm:
        y = _rms(y, fg_ref[...])
    o_ref[...] = y


def _ffn(x2, g, wup, cw, cb, wdn, fg, final_norm):
    t = x2.shape[0]
    const = lambda i: (0, 0)
    whole = lambda a: pl.BlockSpec(a.shape, const)
    halo_blocks = ROW_TILE // HALO
    return pl.pallas_call(
        functools.partial(_ffn_kernel, final_norm=final_norm),
        out_shape=jax.ShapeDtypeStruct((t, D_MODEL), F32),
        grid=(t // ROW_TILE,),
        in_specs=[
            pl.BlockSpec((ROW_TILE, D_MODEL), lambda i: (i, 0)),
            pl.BlockSpec((HALO, D_MODEL), lambda i: (jnp.maximum(i * halo_blocks - 1, 0), 0)),
            whole(g),
            pl.BlockSpec(wup.shape, const, pipeline_mode=pl.Buffered(1)),
            whole(cw), whole(cb),
            pl.BlockSpec(wdn.shape, const, pipeline_mode=pl.Buffered(1)),
            whole(fg),
        ],
        out_specs=pl.BlockSpec((ROW_TILE, D_MODEL), lambda i: (i, 0)),
        scratch_shapes=[pltpu.VMEM((ROW_TILE + HALO, D_MODEL), BF16), pltpu.VMEM((ROW_TILE, D_FF), BF16)],
        compiler_params=_cparams(("parallel",)),
        name="conv_ffn",
    )(x2, x2, g, wup, cw, cb, wdn, fg)


def _nsa_slot_perm():
    cols = []
    for j in range(NSA_HEADS // 2):
        for e in range(2):
            h = _nsa_head(j, e)
            cols += list(range(h * HEAD_DIM, (h + 1) * HEAD_DIM))
    return np.asarray(cols)


def _constants():
    perm = _nsa_slot_perm()
    eg = np.zeros((LANES, 3 * NSA_W), np.float32)
    for newcol, oldcol in enumerate(perm):
        h = oldcol // HEAD_DIM
        for br in range(3):
            eg[SMALL_GC + h * 3 + br, br * NSA_W + newcol] = 1.0
    ovt = np.zeros((N_SEL, LANES), np.float32)
    n_cmp = (SEQ - NSA_CMP_BLOCK) // NSA_CMP_STRIDE + 1
    for c in range(n_cmp):
        cs, ce = c * NSA_CMP_STRIDE, c * NSA_CMP_STRIDE + NSA_CMP_BLOCK - 1
        for jb in range(N_SEL):
            if cs < jb * NSA_SEL_BLOCK + NSA_SEL_BLOCK and ce >= jb * NSA_SEL_BLOCK:
                ovt[jb, c] = 1.0
    pos = np.arange(SEQ)
    kx = np.zeros((SEQ, LANES), np.float32)
    for g in range(NSA_KV_GROUPS):
        kx[pos, g * N_SEL + pos // NSA_SEL_BLOCK] = 1.0
    kx[:, XL_POS:XL_POS + 3] = (256 * (pos // 256))[:, None]
    kx[:, XL_POS + 3:XL_POS + 6] = (pos % 256)[:, None]
    qx = np.zeros((8, LANES), np.float32)
    for hs in range(NSA_HEADS):
        s = np.float32(NSA_SLOPES[_nsa_head(hs // 2, hs % 2)])
        hi = np.float32(np.asarray(s, dtype=BF16))
        mid = np.float32(np.asarray(s - hi, dtype=BF16))
        lo = np.float32(np.asarray(s - hi - mid, dtype=BF16))
        qx[hs, XL_POS:XL_POS + 6] = [hi, mid, lo, hi, mid, lo]
    return (jnp.asarray(eg, BF16), jnp.asarray(ovt, BF16), jnp.asarray(kx, BF16), jnp.asarray(qx, F32))


def _in_offsets():
    o = np.cumsum([0, FOX_W, FOX_W, FOX_W, FOX_HEADS, DIL_W, DIL_W, DIL_W, NSA_W,
                   NSA_KV_W, NSA_KV_W, NSA_KV_W, NSA_KV_W, NSA_KV_W, NSA_KV_W, NSA_HEADS * 3, 3 * D_MODEL])
    return [int(v) for v in o]


D_IN = _in_offsets()[-1]
PACK_ROWS = 256


def _pack_kernel(w_ref, qkv_ref, gate_ref, small_ref):
    (o_qa, _, _, o_fa, o_qb, _, _, o_qc, o_kc, _, o_ks, _, _, _, o_gc, o_g, _) = _in_offsets()
    heads = [_nsa_head(j, e) for j in range(NSA_HEADS // 2) for e in range(2)]

    def put(dst, src, width):
        qkv_ref[:, dst:dst + width] = w_ref[:, src:src + width].astype(BF16)

    put(COL_FOX_Q, o_qa, 3 * FOX_W)
    put(COL_DIL_Q, o_qb, 3 * DIL_W)
    for j in range(NSA_HEADS // 2):
        pair = jnp.concatenate([w_ref[:, o_qc + h * HEAD_DIM:o_qc + (h + 1) * HEAD_DIM] for h in heads[2 * j:2 * j + 2]],
                               axis=1)
        qkv_ref[:, COL_NSA_Q + j * LANES:COL_NSA_Q + (j + 1) * LANES] = pair.astype(BF16)
    put(COL_NSA_KSEL, o_ks, 4 * NSA_KV_W)
    put(COL_NSA_CMP, o_kc, 2 * NSA_KV_W)
    gate_ref[...] = w_ref[:, o_g:o_g + 3 * D_MODEL].astype(BF16)
    pad = jnp.zeros((w_ref.shape[0], LANES - FOX_HEADS - NSA_HEADS * 3), F32)
    small_ref[...] = jnp.concatenate([w_ref[:, o_fa:o_fa + FOX_HEADS], w_ref[:, o_gc:o_g], pad], axis=1).astype(BF16)


def _pack_in_weights(w_in, l):
    rows = w_in.shape[1]
    tiled = lambda width: pl.BlockSpec((PACK_ROWS, width), lambda i: (i, 0))
    return pl.pallas_call(
        _pack_kernel,
        out_shape=(jax.ShapeDtypeStruct((rows, QKV_W), BF16), jax.ShapeDtypeStruct((rows, 3 * D_MODEL), BF16),
                   jax.ShapeDtypeStruct((rows, LANES), BF16)),
        grid=(rows // PACK_ROWS,),
        in_specs=[pl.BlockSpec((None, PACK_ROWS, D_IN), lambda i: (l, i, 0))],
        out_specs=(tiled(QKV_W), tiled(3 * D_MODEL), tiled(LANES)),
        compiler_params=_cparams(("parallel",)),
        name="pack_in_weights",
    )(w_in)


def _layer_params(l, w_in, b_in, cmp_pe, cmp_k_w1, cmp_k_w2, cmp_v_w1, cmp_v_w2, w_pc):
    (o_qa, _, _, o_fa, o_qb, _, _, o_qc, o_kc, _, o_ks, _, _, _, o_gc, o_g, _) = _in_offsets()
    b = b_in[l]
    heads = [_nsa_head(j, e) for j in range(NSA_HEADS // 2) for e in range(2)]
    pieces = ([(o_qa, o_qa + 3 * FOX_W), (o_qb, o_qb + 3 * DIL_W)]
              + [(o_qc + h * HEAD_DIM, o_qc + (h + 1) * HEAD_DIM) for h in heads]
              + [(o_ks, o_ks + 4 * NSA_KV_W), (o_kc, o_kc + 2 * NSA_KV_W)])
    w_qkv, w_gate, w_small = _pack_in_weights(w_in, l)
    pad = LANES - FOX_HEADS - NSA_HEADS * 3
    p = {
        "w_qkv": w_qkv,
        "b_qkv": jnp.concatenate([b[s:e] for s, e in pieces])[None, :],
        "w_small": w_small,
        "b_small": jnp.concatenate([b[o_fa:o_fa + FOX_HEADS], b[o_gc:o_g], jnp.zeros((pad,), b.dtype)])[None, :],
        "w_gate": w_gate,
        "b_gate": b[o_g:][None, :],
        "w_pc": jnp.concatenate([w_pc[l][h * HEAD_DIM:(h + 1) * HEAD_DIM] for h in heads], axis=0).astype(BF16),
    }
    pe = cmp_pe[l].reshape(2, NSA_CMP_STRIDE, 1, HEAD_DIM)
    pe = jnp.broadcast_to(pe, (2, NSA_CMP_STRIDE, NSA_KV_GROUPS, HEAD_DIM)).reshape(2, NSA_CMP_STRIDE * LANES)
    p["pe2"] = jnp.concatenate([pe, jnp.zeros((PE_ROWS - 2, NSA_CMP_STRIDE * LANES), pe.dtype)]).astype(BF16)
    for name, w1, w2 in (("k", cmp_k_w1[l], cmp_k_w2[l]), ("v", cmp_v_w1[l], cmp_v_w2[l])):
        for half, tag in enumerate("ab"):
            wh = w1[half * NSA_CMP_STRIDE * HEAD_DIM:(half + 1) * NSA_CMP_STRIDE * HEAD_DIM]
            wh = wh.reshape(NSA_CMP_STRIDE, HEAD_DIM, NSA_CMP_HIDDEN)
            z = jnp.zeros_like(wh)
            bd = jnp.stack([jnp.concatenate([wh, z], 2), jnp.concatenate([z, wh], 2)], axis=1)
            p["w1" + tag + name] = bd.reshape(NSA_CMP_STRIDE * LANES, NSA_KV_GROUPS * NSA_CMP_HIDDEN).astype(BF16)
        z = jnp.zeros_like(w2)
        p["w2" + name] = jnp.concatenate([jnp.concatenate([w2, z], 1), jnp.concatenate([z, w2], 1)], 0).astype(BF16)
    return p


def kernel(x, norm1_g, w_in, b_in, cmp_pe, cmp_k_w1, cmp_k_w2, cmp_v_w1, cmp_v_w2, w_pa, w_pb, w_pc, w_o,
           norm2_g, w_up, conv_w, conv_b, w_down, final_g):
    bsz, seq, d = x.shape
    assert (seq, d) == (SEQ, D_MODEL)
    depth = w_in.shape[0]
    eg, ovt, kx, qx = _constants()
    x2 = x.reshape(bsz * seq, d)
    for l in range(depth):
        p = _layer_params(l, w_in, b_in, cmp_pe, cmp_k_w1, cmp_k_w2, cmp_v_w1, cmp_v_w2, w_pc)
        g1 = norm1_g[l][None, :]
        qkv, small = _inproj(x2, g1, p["w_qkv"], p["b_qkv"], p["w_small"], p["b_small"])
        qkv3 = qkv.reshape(bsz, seq, QKV_W)
        small3 = small.reshape(bsz, seq, LANES)
        ya = _fox(qkv3, small3)
        yb = _dil(qkv3)
        kc2, vc2 = _compress(qkv3, p["pe2"], p["w1ak"], p["w1bk"], p["w2k"], p["w1av"], p["w1bv"], p["w2v"])
        yc = _nsa(qkv3, kc2, vc2, small3, eg, ovt, kx, qx)
        x2 = _merge(x2, g1, ya.reshape(-1, FOX_W), yb.reshape(-1, DIL_W), yc.reshape(-1, NSA_W),
                    p["w_gate"], p["b_gate"], w_pa[l].astype(BF16), w_pb[l].astype(BF16), p["w_pc"],
                    w_o[l].astype(BF16))
        x2 = _ffn(x2, norm2_g[l][None, :], w_up[l].astype(BF16), conv_w[l], conv_b[l][None, :],
                  w_down[l].astype(BF16), final_g[None, :], final_norm=(l == depth - 1))
    return x2.reshape(bsz, seq, d)
```

```python
import functools

import numpy as np
import jax
import jax.numpy as jnp
from jax import lax
from jax.experimental import pallas as pl
from jax.experimental.pallas import tpu as pltpu

D_MODEL = 1024
SEQ = 2048
HEAD_DIM = 64
RMS_EPS = 1e-6
NEG_INF = -1e30
TINY = 1e-30

FOX_HEADS = 4
DIL_WINDOWS = (128, 512, 2048)
DIL_RATES = (1, 4, 16)
DIL_HEADS = 6
NSA_HEADS = 6
NSA_KV_GROUPS = 2
NSA_CMP_BLOCK = 32
NSA_CMP_STRIDE = 16
NSA_CMP_HIDDEN = 256
NSA_SEL_BLOCK = 64
NSA_SEL_TOPK = 8
NSA_WINDOW = 512
FORCE_SCORE = 1e6
ALIBI_HEADS = DIL_HEADS + NSA_HEADS
D_FF = 2816
CONV_WIDTH = 3

FOX_W = FOX_HEADS * HEAD_DIM
DIL_W = DIL_HEADS * HEAD_DIM
NSA_W = NSA_HEADS * HEAD_DIM
NSA_KV_W = NSA_KV_GROUPS * HEAD_DIM
N_SEL = SEQ // NSA_SEL_BLOCK
N_CMP_PAD = SEQ // NSA_CMP_STRIDE

LANES = 128
VMEM_LIMIT_BYTES = 56 * 1024 * 1024

QKV_W = 3072
COL_FOX_Q, COL_FOX_K, COL_FOX_V = 0, 256, 512
COL_DIL_Q, COL_DIL_K, COL_DIL_V = 768, 1152, 1536
COL_NSA_Q = 1920
COL_NSA_KSEL, COL_NSA_VSEL, COL_NSA_KWIN, COL_NSA_VWIN = 2304, 2432, 2560, 2688
COL_NSA_CMP = 2816
SMALL_FA = 0
SMALL_GC = 4

ROW_TILE = 512
FF_CHUNK = 256
HALO = 16

ATT_TQ = 256
ATT_TK = 512
WIN_SPAN = NSA_WINDOW + ATT_TQ
DIL_BAND = DIL_WINDOWS[0] // DIL_RATES[0]
DIL_SPAN = ATT_TQ + DIL_BAND

F32 = jnp.float32
BF16 = jnp.bfloat16


def _alibi_slopes_np(n):
    return np.exp2(-8.0 * np.arange(1, n + 1, dtype=np.float64) / n)


_SLOPES = _alibi_slopes_np(ALIBI_HEADS)
DIL_SLOPES = [float(np.float32(s)) for s in _SLOPES[0::2]]
NSA_SLOPES = [float(np.float32(s)) for s in _SLOPES[1::2]]


def _cparams(sem):
    return pltpu.CompilerParams(dimension_semantics=sem, vmem_limit_bytes=VMEM_LIMIT_BYTES)


def _rms(x, g):
    return x * lax.rsqrt(jnp.mean(x * x, axis=-1, keepdims=True) + RMS_EPS) * g


def _gelu_tanh(x):
    return 0.5 * x * (1.0 + jnp.tanh(np.float32(np.sqrt(2.0 / np.pi)) * (x + 0.044715 * (x * x * x))))


def _dot(a, b):
    return jnp.dot(a, b, preferred_element_type=F32)


def _dot_nt(a, b):
    return lax.dot_general(a, b, (((1,), (1,)), ((), ())), preferred_element_type=F32)


def _split3(x):
    hi = x.astype(BF16)
    r = x - hi.astype(F32)
    mid = r.astype(BF16)
    lo = (r - mid.astype(F32)).astype(BF16)
    return hi, mid, lo


def _inproj_kernel(x_ref, g_ref, w_ref, b_ref, ws_ref, bs_ref, qkv_ref, small_ref):
    h = _rms(x_ref[...], g_ref[...]).astype(BF16)
    for c in range(0, QKV_W, 512):
        qkv_ref[:, c:c + 512] = (_dot(h, w_ref[:, c:c + 512]) + b_ref[:, c:c + 512]).astype(BF16)
    small_ref[...] = _dot(h, ws_ref[...]) + bs_ref[...]


def _inproj(x2, g, w, b, ws, bs):
    t = x2.shape[0]
    const = lambda i: (0, 0)
    return pl.pallas_call(
        _inproj_kernel,
        out_shape=(jax.ShapeDtypeStruct((t, QKV_W), BF16), jax.ShapeDtypeStruct((t, LANES), F32)),
        grid=(t // ROW_TILE,),
        in_specs=[
            pl.BlockSpec((ROW_TILE, D_MODEL), lambda i: (i, 0)),
            pl.BlockSpec((1, D_MODEL), const),
            pl.BlockSpec((D_MODEL, QKV_W), const),
            pl.BlockSpec((1, QKV_W), const),
            pl.BlockSpec((D_MODEL, LANES), const),
            pl.BlockSpec((1, LANES), const),
        ],
        out_specs=(pl.BlockSpec((ROW_TILE, QKV_W), lambda i: (i, 0)),
                   pl.BlockSpec((ROW_TILE, LANES), lambda i: (i, 0))),
        compiler_params=_cparams(("parallel",)),
        name="inproj",
    )(x2, g, w, b, ws, bs)


def _own_and_ones(x, low_half):
    lane = lax.broadcasted_iota(jnp.int32, x.shape, 1)
    own = (lane < HEAD_DIM) if low_half else (lane >= HEAD_DIM)
    return jnp.where(own, x, jnp.ones_like(x))


def _own_and_zeros(x, low_half):
    lane = lax.broadcasted_iota(jnp.int32, x.shape, 1)
    own = (lane < HEAD_DIM) if low_half else (lane >= HEAD_DIM)
    return jnp.where(own, x, jnp.zeros_like(x))


def _softmax_step(chunks, off, m_old, v_tile):
    mx = chunks[0]
    for sc in chunks[1:]:
        mx = jnp.maximum(mx, sc)
    m_new = jnp.maximum(m_old, jnp.max(mx, axis=1, keepdims=True) + off)
    shift = m_new - off
    p = jnp.concatenate([jnp.exp(sc - shift).astype(BF16) for sc in chunks], axis=1)
    return m_new, jnp.exp(m_old - m_new), _dot(p, v_tile)


def _diag_tile(tile, qi, row0):
    assert ATT_TK == 2 * ATT_TQ

    @pl.when(qi % 2 == 0)
    def _():
        tile(row0, ATT_TQ, True)

    @pl.when(qi % 2 == 1)
    def _():
        tile(row0 - ATT_TQ, ATT_TK, True)


def _normalise(acc):
    return acc / jnp.maximum(pltpu.roll(acc, HEAD_DIM, 1), TINY)


FOX_XROWS = 16


def _rows_to_lanes(x):
    full = jnp.concatenate([x, jnp.zeros((LANES - FOX_XROWS, SEQ), F32)], axis=0)
    return full.T.astype(BF16)


def _fox_kernel(q_ref, k_ref, v_ref, small_ref, o_ref, qaug_ref, kaug_ref, vaug_ref, qst_ref, m_ref, acc_ref):
    qi = pl.program_id(1)
    tq, tk = ATT_TQ, ATT_TK
    nslot = FOX_HEADS // 2

    @pl.when(qi == 0)
    def _():
        fa = small_ref[...].T[0:8, :]
        ls = jnp.minimum(fa, 0.0) - jnp.log(1.0 + jnp.exp(-jnp.abs(fa)))
        pos = lax.broadcasted_iota(jnp.int32, ls.shape, 1)
        c = ls
        sh = 1
        while sh < SEQ:
            c = c + jnp.where(pos >= sh, pltpu.roll(c, sh, 1), 0.0)
            sh *= 2
        pieces = [p.astype(F32) for p in _split3(c)]
        xrow = lax.broadcasted_iota(jnp.int32, (FOX_XROWS, SEQ), 0)
        for j in range(nslot):
            kx = jnp.where((xrow >= 6) & (xrow < 12), 1.0, 0.0)
            for e in range(2):
                h = 2 * j + e
                qx = jnp.where((xrow >= 3 * e) & (xrow < 3 * e + 3), -1.0, 0.0)
                for i, piece in enumerate(pieces):
                    kx = jnp.where(xrow == 3 * e + i, piece[h:h + 1, :], kx)
                    qx = jnp.where(xrow == 6 + 3 * e + i, piece[h:h + 1, :], qx)
                qaug_ref[h] = _rows_to_lanes(qx)
            kaug_ref[j] = _rows_to_lanes(kx)
            v2 = v_ref[:, j * LANES:(j + 1) * LANES]
            vaug_ref[2 * j] = _own_and_ones(v2, True)
            vaug_ref[2 * j + 1] = _own_and_ones(v2, False)

    row0 = pl.multiple_of(qi * tq, tq)
    scale = jnp.asarray(HEAD_DIM ** -0.5, BF16)
    lane = lax.broadcasted_iota(jnp.int32, (tq, LANES), 1)
    rloc = lax.broadcasted_iota(jnp.int32, (tq, LANES), 0)
    for j in range(nslot):
        q2 = q_ref[:, j * LANES:(j + 1) * LANES] * scale
        for e in range(2):
            qst_ref[j, e * tq:(e + 1) * tq, 0:LANES] = _own_and_zeros(q2, e == 0)
            qst_ref[j, e * tq:(e + 1) * tq, LANES:2 * LANES] = qaug_ref[2 * j + e, pl.ds(row0, tq), :]
    m_ref[...] = jnp.full(m_ref.shape, NEG_INF, F32)
    acc_ref[...] = jnp.zeros(acc_ref.shape, F32)

    def tile(start, width, diag):
        start = pl.multiple_of(start, tq)
        for j in range(nslot):
            keys = jnp.concatenate([k_ref[pl.ds(start, width), j * LANES:(j + 1) * LANES],
                                    kaug_ref[j, pl.ds(start, width), :]], axis=1)
            s_slot = _dot_nt(qst_ref[j], keys)
            for e in range(2):
                h = 2 * j + e
                chunks = []
                for ck in range(width // LANES):
                    sc = s_slot[e * tq:(e + 1) * tq, ck * LANES:(ck + 1) * LANES]
                    if diag:
                        sc = jnp.where((lane - rloc) <= (row0 - start - ck * LANES), sc, NEG_INF)
                    chunks.append(sc)
                m_new, alpha, pv = _softmax_step(chunks, 0.0, m_ref[h], vaug_ref[h, pl.ds(start, width), :])
                m_ref[h] = m_new
                acc_ref[h] = alpha * acc_ref[h] + pv

    nfull = (qi * tq) // tk

    def body(kb, carry):
        tile(kb * tk, tk, False)
        return carry

    lax.fori_loop(0, nfull, body, 0)
    _diag_tile(tile, qi, row0)

    for j in range(nslot):
        o_ref[:, j * LANES:(j + 1) * LANES] = jnp.where(
            lane < HEAD_DIM, _normalise(acc_ref[2 * j]), _normalise(acc_ref[2 * j + 1])).astype(BF16)


def _fox(qkv3, small3):
    b = qkv3.shape[0]
    return pl.pallas_call(
        _fox_kernel,
        out_shape=jax.ShapeDtypeStruct((b, SEQ, FOX_W), BF16),
        grid=(b, SEQ // ATT_TQ),
        in_specs=[
            pl.BlockSpec((None, ATT_TQ, FOX_W), lambda bi, qi: (bi, qi, COL_FOX_Q // FOX_W)),
            pl.BlockSpec((None, SEQ, FOX_W), lambda bi, qi: (bi, 0, COL_FOX_K // FOX_W)),
            pl.BlockSpec((None, SEQ, FOX_W), lambda bi, qi: (bi, 0, COL_FOX_V // FOX_W)),
            pl.BlockSpec((None, SEQ, LANES), lambda bi, qi: (bi, 0, 0)),
        ],
        out_specs=pl.BlockSpec((None, ATT_TQ, FOX_W), lambda bi, qi: (bi, qi, 0)),
        scratch_shapes=[
            pltpu.VMEM((FOX_HEADS, SEQ, LANES), BF16),
            pltpu.VMEM((FOX_HEADS // 2, SEQ, LANES), BF16),
            pltpu.VMEM((FOX_HEADS, SEQ, LANES), BF16),
            pltpu.VMEM((FOX_HEADS // 2, 2 * ATT_TQ, 2 * LANES), BF16),
            pltpu.VMEM((FOX_HEADS, ATT_TQ, LANES), F32),
            pltpu.VMEM((FOX_HEADS, ATT_TQ, LANES), F32),
        ],
        compiler_params=_cparams(("parallel", "arbitrary")),
        name="fox_attention",
    )(qkv3, qkv3, qkv3, small3)


def _dil_bias(g, e, delta):
    tq = ATT_TQ
    r = DIL_RATES[g]
    n = SEQ // r
    span = DIL_SPAN if n > tq else tq
    row = lax.broadcasted_iota(jnp.int32, (tq, span), 0)
    col = lax.broadcasted_iota(jnp.int32, (tq, span), 1)
    d = (row - col) + delta
    ok = (d >= 0) & (d <= DIL_BAND)
    if n <= tq:
        ok = ok & ((row // n) == (col // n))
    return jnp.where(ok, (-DIL_SLOPES[2 * g + e] * r) * d.astype(F32), NEG_INF)


def _dil_kernel(q_ref, k_ref, v_ref, o_ref, stage_ref, qd_ref, kd_ref, vaug_ref, out32_ref, lse_ref,
                bias_ref, bias_last_ref):
    tq = ATT_TQ
    ngroup = len(DIL_RATES)
    scale = jnp.asarray(HEAD_DIM ** -0.5, BF16)

    @pl.when(pl.program_id(0) == 0)
    def _():
        for g in range(ngroup - 1):
            for e in range(2):
                bias_ref[g, e, 0] = _dil_bias(g, e, 0)
                bias_ref[g, e, 1] = _dil_bias(g, e, DIL_BAND)
        for e in range(2):
            bias_last_ref[e] = _dil_bias(ngroup - 1, e, 0)

    for g, r in enumerate(DIL_RATES):
        n = SEQ // r
        cols = slice(g * LANES, (g + 1) * LANES)
        for which, src in enumerate((q_ref, k_ref, v_ref)):
            x = src[:, cols]
            if which == 0:
                x = x * scale
            if r > 1:
                stage_ref[...] = x.astype(F32)
                x = jnp.concatenate([stage_ref[pl.ds(rho, n, stride=r), :] for rho in range(r)], axis=0).astype(BF16)
            if which == 0:
                qd_ref[g, 0] = _own_and_zeros(x, True)
                qd_ref[g, 1] = _own_and_zeros(x, False)
            elif which == 1:
                kd_ref[g] = x
            else:
                vaug_ref[g, 0] = _own_and_ones(x, True)
                vaug_ref[g, 1] = _own_and_ones(x, False)

    low = lax.broadcasted_iota(jnp.int32, (tq, LANES), 1) < HEAD_DIM
    assert all(SEQ // r > tq for r in DIL_RATES[:-1]) and SEQ // DIL_RATES[-1] <= tq

    def tile(i):
        u0 = pl.multiple_of(i * tq, tq)
        for g, r in enumerate(DIL_RATES):
            n = SEQ // r
            span = DIL_SPAN if n > tq else tq
            if n > tq:
                cstart = (u0 // n) * n
                kstart = pl.multiple_of(cstart + jnp.maximum(u0 - cstart - DIL_BAND, 0), LANES)
            else:
                kstart = u0
            later = jnp.minimum(u0 - kstart, 1)
            k2 = kd_ref[g, pl.ds(kstart, span), :]
            halves, lses = [], []
            for e in range(2):
                s = _dot_nt(qd_ref[g, e, pl.ds(u0, tq), :], k2)
                chunks = []
                for ck in range(span // LANES):
                    cols = slice(ck * LANES, (ck + 1) * LANES)
                    bias = bias_ref[g, e, later, :, cols] if n > tq else bias_last_ref[e, :, cols]
                    chunks.append(s[:, cols] + bias)
                mx = chunks[0]
                for sc in chunks[1:]:
                    mx = jnp.maximum(mx, sc)
                m = jnp.broadcast_to(jnp.max(mx, axis=1, keepdims=True), (tq, LANES))
                p = jnp.concatenate([jnp.exp(sc - m).astype(BF16) for sc in chunks], axis=1)
                pv = _dot(p, vaug_ref[g, e, pl.ds(kstart, span), :])
                l = jnp.maximum(pltpu.roll(pv, HEAD_DIM, 1), TINY)
                halves.append(pv / l)
                lses.append(m + jnp.log(l))
            o_slot = jnp.where(low, halves[0], halves[1])
            lse_slot = jnp.where(low, lses[0], lses[1])
            if r == 1:
                out32_ref[g, pl.ds(u0, tq), :] = o_slot
                lse_ref[g, pl.ds(u0, tq), :] = lse_slot
            elif n > tq:
                rho = u0 // n
                t0 = (u0 - rho * n) * r + rho
                out32_ref[g, pl.ds(t0, tq, stride=r), :] = o_slot
                lse_ref[g, pl.ds(t0, tq, stride=r), :] = lse_slot
            else:
                for cc in range(tq // n):
                    rho = u0 // n + cc
                    out32_ref[g, pl.ds(rho, n, stride=r), :] = o_slot[cc * n:(cc + 1) * n, :]
                    lse_ref[g, pl.ds(rho, n, stride=r), :] = lse_slot[cc * n:(cc + 1) * n, :]

    def two_tiles(i, carry):
        tile(2 * i)
        tile(2 * i + 1)
        return carry

    lax.fori_loop(0, SEQ // (2 * tq), two_tiles, 0)

    top = jnp.maximum(jnp.maximum(lse_ref[0], lse_ref[1]), lse_ref[2])
    ws = [jnp.exp(lse_ref[g] - top) for g in range(ngroup)]
    den = ws[0] + ws[1] + ws[2]
    for g in range(ngroup):
        o_ref[:, g * LANES:(g + 1) * LANES] = (out32_ref[g] * (ws[g] / den)).astype(BF16)


def _dil(qkv3):
    b = qkv3.shape[0]
    ngroup = len(DIL_RATES)
    blk = lambda col: pl.BlockSpec((None, SEQ, DIL_W), lambda bi: (bi, 0, col // DIL_W))
    return pl.pallas_call(
        _dil_kernel,
        out_shape=jax.ShapeDtypeStruct((b, SEQ, DIL_W), BF16),
        grid=(b,),
        in_specs=[blk(COL_DIL_Q), blk(COL_DIL_K), blk(COL_DIL_V)],
        out_specs=pl.BlockSpec((None, SEQ, DIL_W), lambda bi: (bi, 0, 0)),
        scratch_shapes=[
            pltpu.VMEM((SEQ, LANES), F32),
            pltpu.VMEM((ngroup, 2, SEQ, LANES), BF16),
            pltpu.VMEM((ngroup, SEQ, LANES), BF16),
            pltpu.VMEM((ngroup, 2, SEQ, LANES), BF16),
            pltpu.VMEM((ngroup, SEQ, LANES), F32),
            pltpu.VMEM((ngroup, SEQ, LANES), F32),
            pltpu.VMEM((ngroup - 1, 2, 2, ATT_TQ, DIL_SPAN), F32),
            pltpu.VMEM((2, ATT_TQ, ATT_TQ), F32),
        ],
        compiler_params=_cparams(("arbitrary",)),
        name="dilated_attention",
    )(qkv3, qkv3, qkv3)


PE_ROWS = 16


def _cmp_kernel(x_ref, pe_ref, w1ak_ref, w1bk_ref, w2k_ref, w1av_ref, w1bv_ref, w2v_ref, kc_ref, vc_ref,
                stage_ref, xp_ref):
    xp_ref[N_CMP_PAD:, :] = pe_ref[...]
    for kind, (w1a_ref, w1b_ref, w2_ref, out_ref) in enumerate(
            ((w1ak_ref, w1bk_ref, w2k_ref, kc_ref), (w1av_ref, w1bv_ref, w2v_ref, vc_ref))):
        stage_ref[...] = x_ref[:, kind * LANES:(kind + 1) * LANES].astype(F32)
        for m in range(NSA_CMP_STRIDE):
            xp_ref[0:N_CMP_PAD, m * LANES:(m + 1) * LANES] = stage_ref[
                pl.ds(m, N_CMP_PAD, stride=NSA_CMP_STRIDE), :].astype(BF16)
        xp = xp_ref[...]
        a = _dot(xp, w1a_ref[...])
        b = _dot(xp, w1b_ref[...])
        cpe = a[N_CMP_PAD:N_CMP_PAD + 1, :] + b[N_CMP_PAD + 1:N_CMP_PAD + 2, :]
        bnext = pltpu.roll(b[0:N_CMP_PAD, :], N_CMP_PAD - 1, 0)
        h1 = _gelu_tanh(a[0:N_CMP_PAD, :] + bnext + cpe)
        out_ref[...] = _dot(h1.astype(BF16), w2_ref[...]).astype(BF16)


def _compress(qkv3, pe2, w1ak, w1bk, w2k, w1av, w1bv, w2v):
    b = qkv3.shape[0]
    whole = lambda a: pl.BlockSpec(a.shape, lambda bi: (0, 0))
    cmp_w = 2 * NSA_KV_W
    return pl.pallas_call(
        _cmp_kernel,
        out_shape=(jax.ShapeDtypeStruct((b, N_CMP_PAD, LANES), BF16),) * 2,
        grid=(b,),
        in_specs=[pl.BlockSpec((None, SEQ, cmp_w), lambda bi: (bi, 0, COL_NSA_CMP // cmp_w)),
                  whole(pe2), whole(w1ak), whole(w1bk), whole(w2k), whole(w1av), whole(w1bv), whole(w2v)],
        out_specs=(pl.BlockSpec((None, N_CMP_PAD, LANES), lambda bi: (bi, 0, 0)),) * 2,
        scratch_shapes=[pltpu.VMEM((SEQ, LANES), F32),
                        pltpu.VMEM((N_CMP_PAD + PE_ROWS, NSA_CMP_STRIDE * LANES), BF16)],
        compiler_params=_cparams(("parallel",)),
        name="nsa_compress",
    )(qkv3, pe2, w1ak, w1bk, w2k, w1av, w1bv, w2v)


def _nsa_head(j, e):
    return j + (NSA_HEADS // NSA_KV_GROUPS) * e


XL_POS = 2 * N_SEL


def _nsa_kernel(q_ref, ksl_ref, vsl_ref, kwn_ref, vwn_ref, kc_ref, vc_ref, small_ref,
                eg_ref, ovt_ref, kx_ref, qx_ref, o_ref,
                vaug_ref, qst_ref, score_ref, selp_ref, m_ref, acc_ref):
    qi = pl.program_id(1)
    tq, tk = ATT_TQ, ATT_TK
    nhs = NSA_HEADS
    row0 = pl.multiple_of(qi * tq, tq)
    lane = lax.broadcasted_iota(jnp.int32, (tq, LANES), 1)
    rloc = lax.broadcasted_iota(jnp.int32, (tq, LANES), 0)
    low = lane < HEAD_DIM
    slope_of = [NSA_SLOPES[_nsa_head(hs // 2, hs % 2)] for hs in range(nhs)]

    @pl.when(qi == 0)
    def _():
        for i, src in enumerate((vsl_ref, vwn_ref)):
            vaug_ref[2 * i] = _own_and_ones(src[...], True)
            vaug_ref[2 * i + 1] = _own_and_ones(src[...], False)

    scale = jnp.asarray(HEAD_DIM ** -0.5, BF16)
    for j in range(nhs // 2):
        q2 = q_ref[:, j * LANES:(j + 1) * LANES] * scale
        for e in range(2):
            hs = 2 * j + e
            qst_ref[hs * tq:(hs + 1) * tq, 0:LANES] = _own_and_zeros(q2, e == 0)
            qst_ref[hs * tq:(hs + 1) * tq, LANES:2 * LANES] = jnp.broadcast_to(
                qx_ref[hs:hs + 1, :], (tq, LANES)).astype(BF16)

    sg = jax.nn.sigmoid(small_ref[...])
    ghi = sg.astype(BF16)
    glo = (sg - ghi.astype(F32)).astype(BF16)
    gexp = _dot(ghi, eg_ref[...]) + _dot(glo, eg_ref[...])

    tpos = row0 + rloc
    tposf = tpos.astype(F32)
    row_off = [-slope_of[hs] * tposf for hs in range(nhs)]

    wstart = pl.multiple_of(jnp.maximum(row0 - NSA_WINDOW, 0), tq)
    delta_w = row0 - wstart
    keys_w = jnp.concatenate([kwn_ref[pl.ds(wstart, WIN_SPAN), :], kx_ref[pl.ds(wstart, WIN_SPAN), :]], axis=1)
    s_w = _dot_nt(qst_ref[...], keys_w)
    in_window = []
    for ck in range(WIN_SPAN // LANES):
        d = (rloc - lane) + (delta_w - ck * LANES)
        in_window.append(jnp.where((d >= 0) & (d < NSA_WINDOW), 0.0, NEG_INF))
    o_win = []
    for hs in range(nhs):
        chunks = [s_w[hs * tq:(hs + 1) * tq, ck * LANES:(ck + 1) * LANES] + in_window[ck]
                  for ck in range(WIN_SPAN // LANES)]
        _, _, pv = _softmax_step(chunks, row_off[hs], jnp.full((tq, LANES), NEG_INF, F32),
                                 vaug_ref[2 + hs % 2, pl.ds(wstart, WIN_SPAN), :])
        o_win.append(pv)

    dist_c = tpos - (lane * NSA_CMP_STRIDE + NSA_CMP_BLOCK - 1)
    valid_c = dist_c >= 0
    dist_cf = dist_c.astype(F32)
    s_c = _dot_nt(qst_ref[:, 0:LANES], kc_ref[...])
    psum = [jnp.zeros((tq, LANES), F32) for _ in range(NSA_KV_GROUPS)]
    o_cmp = []
    for hs in range(nhs):
        s = s_c[hs * tq:(hs + 1) * tq, :] - slope_of[hs] * dist_cf
        s = jnp.where(valid_c, s, NEG_INF)
        m = jnp.max(s, axis=1, keepdims=True)
        p = jnp.where(valid_c, jnp.exp(s - m), 0.0)
        p = p / jnp.maximum(jnp.sum(p, axis=1, keepdims=True), TINY)
        psum[hs % 2] = psum[hs % 2] + p
        o_cmp.append(_dot(p.astype(BF16), vc_ref[...]))

    jrow = lax.broadcasted_iota(jnp.int32, (N_SEL, tq), 0)
    qblk = (row0 + lax.broadcasted_iota(jnp.int32, (N_SEL, tq), 1)) >> 6
    causal = jrow <= qblk
    forced = (jrow == 0) | (jrow == qblk) | (jrow == qblk - 1)
    selp_ref[NSA_KV_GROUPS * N_SEL:, :] = jnp.zeros((LANES - NSA_KV_GROUPS * N_SEL, tq), F32)
    for g in range(NSA_KV_GROUPS):
        imp = sum(_dot_nt(ovt_ref[...], part) for part in _split3(psum[g]))
        score = jnp.where(causal & forced, FORCE_SCORE, jnp.where(causal, imp, -1.0))
        score_ref[g] = score
        rank = jnp.zeros((N_SEL, tq), jnp.int32)
        for jp in range(N_SEL):
            other = score_ref[g, jp:jp + 1, :]
            ahead = (other > score) | ((other == score) & (jp < jrow))
            rank = rank + ahead.astype(jnp.int32)
        sel = (rank < NSA_SEL_TOPK) & causal
        selp_ref[g * N_SEL:(g + 1) * N_SEL, :] = jnp.where(sel, 0.0, NEG_INF)
    sel_t = selp_ref[...].T
    for hs in range(nhs):
        g = hs % 2
        mine = (lane >= g * N_SEL) & (lane < (g + 1) * N_SEL)
        extra = jnp.where(mine, sel_t, 0.0) + qx_ref[hs:hs + 1, :]
        qst_ref[hs * tq:(hs + 1) * tq, LANES:2 * LANES] = extra.astype(BF16)

    m_ref[...] = jnp.full(m_ref.shape, NEG_INF, F32)
    acc_ref[...] = jnp.zeros(acc_ref.shape, F32)

    def sel_tile(start, width, diag):
        start = pl.multiple_of(start, tq)
        keys = jnp.concatenate([ksl_ref[pl.ds(start, width), :], kx_ref[pl.ds(start, width), :]], axis=1)
        s_all = _dot_nt(qst_ref[...], keys)
        for hs in range(nhs):
            chunks = []
            for ck in range(width // LANES):
                sc = s_all[hs * tq:(hs + 1) * tq, ck * LANES:(ck + 1) * LANES]
                if diag:
                    sc = jnp.where((lane - rloc) <= (row0 - start - ck * LANES), sc, NEG_INF)
                chunks.append(sc)
            m_new, alpha, pv = _softmax_step(chunks, row_off[hs], m_ref[hs], vaug_ref[hs % 2, pl.ds(start, width), :])
            m_ref[hs] = m_new
            acc_ref[hs] = alpha * acc_ref[hs] + pv

    nfull = (qi * tq) // tk

    def body(i, carry):
        sel_tile((nfull - 1 - i) * tk, tk, False)
        return carry

    _diag_tile(sel_tile, qi, row0)
    lax.fori_loop(0, nfull, body, 0)

    for j in range(nhs // 2):
        oc = jnp.where(low, o_cmp[2 * j], o_cmp[2 * j + 1])
        osl = jnp.where(low, _normalise(acc_ref[2 * j]), _normalise(acc_ref[2 * j + 1]))
        ow = jnp.where(low, _normalise(o_win[2 * j]), _normalise(o_win[2 * j + 1]))
        g_c = gexp[:, 0 * NSA_W + j * LANES:0 * NSA_W + (j + 1) * LANES]
        g_s = gexp[:, 1 * NSA_W + j * LANES:1 * NSA_W + (j + 1) * LANES]
        g_w = gexp[:, 2 * NSA_W + j * LANES:2 * NSA_W + (j + 1) * LANES]
        o_ref[:, j * LANES:(j + 1) * LANES] = (g_c * oc + g_s * osl + g_w * ow).astype(BF16)


def _nsa(qkv3, kc2, vc2, small3, eg, ovt, kx, qx):
    b = qkv3.shape[0]
    full = lambda col: pl.BlockSpec((None, SEQ, LANES), lambda bi, qi: (bi, 0, col // LANES))
    c2 = lambda bi, qi: (0, 0)
    return pl.pallas_call(
        _nsa_kernel,
        out_shape=jax.ShapeDtypeStruct((b, SEQ, NSA_W), BF16),
        grid=(b, SEQ // ATT_TQ),
        in_specs=[
            pl.BlockSpec((None, ATT_TQ, NSA_W), lambda bi, qi: (bi, qi, COL_NSA_Q // NSA_W)),
            full(COL_NSA_KSEL), full(COL_NSA_VSEL), full(COL_NSA_KWIN), full(COL_NSA_VWIN),
            pl.BlockSpec((None, N_CMP_PAD, LANES), lambda bi, qi: (bi, 0, 0)),
            pl.BlockSpec((None, N_CMP_PAD, LANES), lambda bi, qi: (bi, 0, 0)),
            pl.BlockSpec((None, ATT_TQ, LANES), lambda bi, qi: (bi, qi, 0)),
            pl.BlockSpec((LANES, 3 * NSA_W), c2),
            pl.BlockSpec((N_SEL, LANES), c2),
            pl.BlockSpec((SEQ, LANES), c2),
            pl.BlockSpec((8, LANES), c2),
        ],
        out_specs=pl.BlockSpec((None, ATT_TQ, NSA_W), lambda bi, qi: (bi, qi, 0)),
        scratch_shapes=[
            pltpu.VMEM((4, SEQ, LANES), BF16),
            pltpu.VMEM((NSA_HEADS * ATT_TQ, 2 * LANES), BF16),
            pltpu.VMEM((NSA_KV_GROUPS, N_SEL, ATT_TQ), F32),
            pltpu.VMEM((LANES, ATT_TQ), F32),
            pltpu.VMEM((NSA_HEADS, ATT_TQ, LANES), F32),
            pltpu.VMEM((NSA_HEADS, ATT_TQ, LANES), F32),
        ],
        compiler_params=_cparams(("parallel", "arbitrary")),
        name="nsa_attention",
    )(qkv3, qkv3, qkv3, qkv3, qkv3, kc2, vc2, small3, eg, ovt, kx, qx)


def _merge_kernel(x_ref, g_ref, ya_ref, yb_ref, yc_ref, wg_ref, bg_ref, wpa_ref, wpb_ref, wpc_ref, wo_ref, o_ref):
    x = x_ref[...]
    h = _rms(x, g_ref[...]).astype(BF16)
    merged = None
    for i, (y_ref, wp_ref) in enumerate(((ya_ref, wpa_ref), (yb_ref, wpb_ref), (yc_ref, wpc_ref))):
        cols = slice(i * D_MODEL, (i + 1) * D_MODEL)
        gate = jax.nn.sigmoid(_dot(h, wg_ref[:, cols]) + bg_ref[:, cols])
        term = gate * _dot(y_ref[...], wp_ref[...])
        merged = term if merged is None else merged + term
    o_ref[...] = x + _dot(merged.astype(BF16), wo_ref[...])


def _merge(x2, g, ya, yb, yc, wg, bg, wpa, wpb, wpc, wo):
    t = x2.shape[0]
    const = lambda i: (0, 0)
    rows = lambda w: pl.BlockSpec((ROW_TILE, w), lambda i: (i, 0))
    whole = lambda a: pl.BlockSpec(a.shape, const)
    return pl.pallas_call(
        _merge_kernel,
        out_shape=jax.ShapeDtypeStruct((t, D_MODEL), F32),
        grid=(t // ROW_TILE,),
        in_specs=[rows(D_MODEL), whole(g), rows(FOX_W), rows(DIL_W), rows(NSA_W),
                  whole(wg), whole(bg), whole(wpa), whole(wpb), whole(wpc), whole(wo)],
        out_specs=rows(D_MODEL),
        compiler_params=_cparams(("parallel",)),
        name="merge_oproj",
    )(x2, g, ya, yb, yc, wg, bg, wpa, wpb, wpc, wo)


def _ffn_kernel(x_ref, xh_ref, g_ref, wup_ref, cw_ref, cb_ref, wdn_ref, fg_ref, o_ref, h_ref, act_ref, *, final_norm):
    i = pl.program_id(0)
    x = x_ref[...]
    g = g_ref[...]
    h_ref[0:HALO, :] = _rms(xh_ref[...], g).astype(BF16)
    h_ref[HALO:, :] = _rms(x, g).astype(BF16)
    row = lax.broadcasted_iota(jnp.int32, (ROW_TILE + HALO, FF_CHUNK), 0)
    seq_start = (i % (SEQ // ROW_TILE)) == 0
    keep = jnp.logical_not((row < HALO) & seq_start)
    for c in range(0, D_FF, FF_CHUNK):
        a = jnp.where(keep, _dot(h_ref[...], wup_ref[:, c:c + FF_CHUNK]), 0.0)
        conv = (a * cw_ref[2:3, c:c + FF_CHUNK]
                + pltpu.roll(a, 1, 0) * cw_ref[1:2, c:c + FF_CHUNK]
                + pltpu.roll(a, 2, 0) * cw_ref[0:1, c:c + FF_CHUNK]
                + cb_ref[:, c:c + FF_CHUNK])
        gate = _dot(h_ref[HALO:, :], wup_ref[:, D_FF + c:D_FF + c + FF_CHUNK])
        act_ref[:, c:c + FF_CHUNK] = (_gelu_tanh(conv[HALO:, :]) * gate).astype(BF16)
    y = x + _dot(act_ref[...], wdn_ref[...])
    if final_norm:
        y = _rms(y, fg_ref[...])
    o_ref[...] = y


def _ffn(x2, g, wup, cw, cb, wdn, fg, final_norm):
    t = x2.shape[0]
    const = lambda i: (0, 0)
    whole = lambda a: pl.BlockSpec(a.shape, const)
    halo_blocks = ROW_TILE // HALO
    return pl.pallas_call(
        functools.partial(_ffn_kernel, final_norm=final_norm),
        out_shape=jax.ShapeDtypeStruct((t, D_MODEL), F32),
        grid=(t // ROW_TILE,),
        in_specs=[
            pl.BlockSpec((ROW_TILE, D_MODEL), lambda i: (i, 0)),
            pl.BlockSpec((HALO, D_MODEL), lambda i: (jnp.maximum(i * halo_blocks - 1, 0), 0)),
            whole(g),
            pl.BlockSpec(wup.shape, const, pipeline_mode=pl.Buffered(1)),
            whole(cw), whole(cb),
            pl.BlockSpec(wdn.shape, const, pipeline_mode=pl.Buffered(1)),
            whole(fg),
        ],
        out_specs=pl.BlockSpec((ROW_TILE, D_MODEL), lambda i: (i, 0)),
        scratch_shapes=[pltpu.VMEM((ROW_TILE + HALO, D_MODEL), BF16), pltpu.VMEM((ROW_TILE, D_FF), BF16)],
        compiler_params=_cparams(("parallel",)),
        name="conv_ffn",
    )(x2, x2, g, wup, cw, cb, wdn, fg)


def _nsa_slot_perm():
    cols = []
    for j in range(NSA_HEADS // 2):
        for e in range(2):
            h = _nsa_head(j, e)
            cols += list(range(h * HEAD_DIM, (h + 1) * HEAD_DIM))
    return np.asarray(cols)


def _constants():
    perm = _nsa_slot_perm()
    eg = np.zeros((LANES, 3 * NSA_W), np.float32)
    for newcol, oldcol in enumerate(perm):
        h = oldcol // HEAD_DIM
        for br in range(3):
            eg[SMALL_GC + h * 3 + br, br * NSA_W + newcol] = 1.0
    ovt = np.zeros((N_SEL, LANES), np.float32)
    n_cmp = (SEQ - NSA_CMP_BLOCK) // NSA_CMP_STRIDE + 1
    for c in range(n_cmp):
        cs, ce = c * NSA_CMP_STRIDE, c * NSA_CMP_STRIDE + NSA_CMP_BLOCK - 1
        for jb in range(N_SEL):
            if cs < jb * NSA_SEL_BLOCK + NSA_SEL_BLOCK and ce >= jb * NSA_SEL_BLOCK:
                ovt[jb, c] = 1.0
    pos = np.arange(SEQ)
    kx = np.zeros((SEQ, LANES), np.float32)
    for g in range(NSA_KV_GROUPS):
        kx[pos, g * N_SEL + pos // NSA_SEL_BLOCK] = 1.0
    kx[:, XL_POS:XL_POS + 3] = (256 * (pos // 256))[:, None]
    kx[:, XL_POS + 3:XL_POS + 6] = (pos % 256)[:, None]
    qx = np.zeros((8, LANES), np.float32)
    for hs in range(NSA_HEADS):
        s = np.float32(NSA_SLOPES[_nsa_head(hs // 2, hs % 2)])
        hi = np.float32(np.asarray(s, dtype=BF16))
        mid = np.float32(np.asarray(s - hi, dtype=BF16))
        lo = np.float32(np.asarray(s - hi - mid, dtype=BF16))
        qx[hs, XL_POS:XL_POS + 6] = [hi, mid, lo, hi, mid, lo]
    return (jnp.asarray(eg, BF16), jnp.asarray(ovt, BF16), jnp.asarray(kx, BF16), jnp.asarray(qx, F32))


def _in_offsets():
    o = np.cumsum([0, FOX_W, FOX_W, FOX_W, FOX_HEADS, DIL_W, DIL_W, DIL_W, NSA_W,
                   NSA_KV_W, NSA_KV_W, NSA_KV_W, NSA_KV_W, NSA_KV_W, NSA_KV_W, NSA_HEADS * 3, 3 * D_MODEL])
    return [int(v) for v in o]


D_IN = _in_offsets()[-1]
PACK_ROWS = 256


def _pack_kernel(w_ref, qkv_ref, gate_ref, small_ref):
    (o_qa, _, _, o_fa, o_qb, _, _, o_qc, o_kc, _, o_ks, _, _, _, o_gc, o_g, _) = _in_offsets()
    heads = [_nsa_head(j, e) for j in range(NSA_HEADS // 2) for e in range(2)]

    def put(dst, src, width):
        qkv_ref[:, dst:dst + width] = w_ref[:, src:src + width].astype(BF16)

    put(COL_FOX_Q, o_qa, 3 * FOX_W)
    put(COL_DIL_Q, o_qb, 3 * DIL_W)
    for j in range(NSA_HEADS // 2):
        pair = jnp.concatenate([w_ref[:, o_qc + h * HEAD_DIM:o_qc + (h + 1) * HEAD_DIM] for h in heads[2 * j:2 * j + 2]],
                               axis=1)
        qkv_ref[:, COL_NSA_Q + j * LANES:COL_NSA_Q + (j + 1) * LANES] = pair.astype(BF16)
    put(COL_NSA_KSEL, o_ks, 4 * NSA_KV_W)
    put(COL_NSA_CMP, o_kc, 2 * NSA_KV_W)
    gate_ref[...] = w_ref[:, o_g:o_g + 3 * D_MODEL].astype(BF16)
    pad = jnp.zeros((w_ref.shape[0], LANES - FOX_HEADS - NSA_HEADS * 3), F32)
    small_ref[...] = jnp.concatenate([w_ref[:, o_fa:o_fa + FOX_HEADS], w_ref[:, o_gc:o_g], pad], axis=1).astype(BF16)


def _pack_in_weights(w_in, l):
    rows = w_in.shape[1]
    tiled = lambda width: pl.BlockSpec((PACK_ROWS, width), lambda i: (i, 0))
    return pl.pallas_call(
        _pack_kernel,
        out_shape=(jax.ShapeDtypeStruct((rows, QKV_W), BF16), jax.ShapeDtypeStruct((rows, 3 * D_MODEL), BF16),
                   jax.ShapeDtypeStruct((rows, LANES), BF16)),
        grid=(rows // PACK_ROWS,),
        in_specs=[pl.BlockSpec((None, PACK_ROWS, D_IN), lambda i: (l, i, 0))],
        out_specs=(tiled(QKV_W), tiled(3 * D_MODEL), tiled(LANES)),
        compiler_params=_cparams(("parallel",)),
        name="pack_in_weights",
    )(w_in)


def _layer_params(l, w_in, b_in, cmp_pe, cmp_k_w1, cmp_k_w2, cmp_v_w1, cmp_v_w2, w_pc):
    (o_qa, _, _, o_fa, o_qb, _, _, o_qc, o_kc, _, o_ks, _, _, _, o_gc, o_g, _) = _in_offsets()
    b = b_in[l]
    heads = [_nsa_head(j, e) for j in range(NSA_HEADS // 2) for e in range(2)]
    pieces = ([(o_qa, o_qa + 3 * FOX_W), (o_qb, o_qb + 3 * DIL_W)]
              + [(o_qc + h * HEAD_DIM, o_qc + (h + 1) * HEAD_DIM) for h in heads]
              + [(o_ks, o_ks + 4 * NSA_KV_W), (o_kc, o_kc + 2 * NSA_KV_W)])
    w_qkv, w_gate, w_small = _pack_in_weights(w_in, l)
    pad = LANES - FOX_HEADS - NSA_HEADS * 3
    p = {
        "w_qkv": w_qkv,
        "b_qkv": jnp.concatenate([b[s:e] for s, e in pieces])[None, :],
        "w_small": w_small,
        "b_small": jnp.concatenate([b[o_fa:o_fa + FOX_HEADS], b[o_gc:o_g], jnp.zeros((pad,), b.dtype)])[None, :],
        "w_gate": w_gate,
        "b_gate": b[o_g:][None, :],
        "w_pc": jnp.concatenate([w_pc[l][h * HEAD_DIM:(h + 1) * HEAD_DIM] for h in heads], axis=0).astype(BF16),
    }
    pe = cmp_pe[l].reshape(2, NSA_CMP_STRIDE, 1, HEAD_DIM)
    pe = jnp.broadcast_to(pe, (2, NSA_CMP_STRIDE, NSA_KV_GROUPS, HEAD_DIM)).reshape(2, NSA_CMP_STRIDE * LANES)
    p["pe2"] = jnp.concatenate([pe, jnp.zeros((PE_ROWS - 2, NSA_CMP_STRIDE * LANES), pe.dtype)]).astype(BF16)
    for name, w1, w2 in (("k", cmp_k_w1[l], cmp_k_w2[l]), ("v", cmp_v_w1[l], cmp_v_w2[l])):
        for half, tag in enumerate("ab"):
            wh = w1[half * NSA_CMP_STRIDE * HEAD_DIM:(half + 1) * NSA_CMP_STRIDE * HEAD_DIM]
            wh = wh.reshape(NSA_CMP_STRIDE, HEAD_DIM, NSA_CMP_HIDDEN)
            z = jnp.zeros_like(wh)
            bd = jnp.stack([jnp.concatenate([wh, z], 2), jnp.concatenate([z, wh], 2)], axis=1)
            p["w1" + tag + name] = bd.reshape(NSA_CMP_STRIDE * LANES, NSA_KV_GROUPS * NSA_CMP_HIDDEN).astype(BF16)
        z = jnp.zeros_like(w2)
        p["w2" + name] = jnp.concatenate([jnp.concatenate([w2, z], 1), jnp.concatenate([z, w2], 1)], 0).astype(BF16)
    return p


def kernel(x, norm1_g, w_in, b_in, cmp_pe, cmp_k_w1, cmp_k_w2, cmp_v_w1, cmp_v_w2, w_pa, w_pb, w_pc, w_o,
           norm2_g, w_up, conv_w, conv_b, w_down, final_g):
    bsz, seq, d = x.shape
    assert (seq, d) == (SEQ, D_MODEL)
    depth = w_in.shape[0]
    eg, ovt, kx, qx = _constants()
    x2 = x.reshape(bsz * seq, d)
    for l in range(depth):
        p = _layer_params(l, w_in, b_in, cmp_pe, cmp_k_w1, cmp_k_w2, cmp_v_w1, cmp_v_w2, w_pc)
        g1 = norm1_g[l][None, :]
        qkv, small = _inproj(x2, g1, p["w_qkv"], p["b_qkv"], p["w_small"], p["b_small"])
        qkv3 = qkv.reshape(bsz, seq, QKV_W)
        small3 = small.reshape(bsz, seq, LANES)
        ya = _fox(qkv3, small3)
        yb = _dil(qkv3)
        kc2, vc2 = _compress(qkv3, p["pe2"], p["w1ak"], p["w1bk"], p["w2k"], p["w1av"], p["w1bv"], p["w2v"])
        yc = _nsa(qkv3, kc2, vc2, small3, eg, ovt, kx, qx)
        x2 = _merge(x2, g1, ya.reshape(-1, FOX_W), yb.reshape(-1, DIL_W), yc.reshape(-1, NSA_W),
                    p["w_gate"], p["b_gate"], w_pa[l].astype(BF16), w_pb[l].astype(BF16), p["w_pc"],
                    w_o[l].astype(BF16))
        x2 = _ffn(x2, norm2_g[l][None, :], w_up[l].astype(BF16), conv_w[l], conv_b[l][None, :],
                  w_down[l].astype(BF16), final_g[None, :], final_norm=(l == depth - 1))
    return x2.reshape(bsz, seq, d)
```

```python
import functools

import numpy as np
import jax
import jax.numpy as jnp
from jax import lax
from jax.experimental import pallas as pl
from jax.experimental.pallas import tpu as pltpu

D_MODEL = 1024
SEQ = 2048
HEAD_DIM = 64
RMS_EPS = 1e-6
NEG_INF = -1e30
TINY = 1e-30

FOX_HEADS = 4
DIL_WINDOWS = (128, 512, 2048)
DIL_RATES = (1, 4, 16)
DIL_HEADS = 6
NSA_HEADS = 6
NSA_KV_GROUPS = 2
NSA_CMP_BLOCK = 32
NSA_CMP_STRIDE = 16
NSA_CMP_HIDDEN = 256
NSA_SEL_BLOCK = 64
NSA_SEL_TOPK = 8
NSA_WINDOW = 512
FORCE_SCORE = 1e6
ALIBI_HEADS = DIL_HEADS + NSA_HEADS
D_FF = 2816
CONV_WIDTH = 3

FOX_W = FOX_HEADS * HEAD_DIM
DIL_W = DIL_HEADS * HEAD_DIM
NSA_W = NSA_HEADS * HEAD_DIM
NSA_KV_W = NSA_KV_GROUPS * HEAD_DIM
N_SEL = SEQ // NSA_SEL_BLOCK
N_CMP_PAD = SEQ // NSA_CMP_STRIDE

LANES = 128
VMEM_LIMIT_BYTES = 56 * 1024 * 1024

QKV_W = 3072
COL_FOX_Q, COL_FOX_K, COL_FOX_V = 0, 256, 512
COL_DIL_Q, COL_DIL_K, COL_DIL_V = 768, 1152, 1536
COL_NSA_Q = 1920
COL_NSA_KSEL, COL_NSA_VSEL, COL_NSA_KWIN, COL_NSA_VWIN = 2304, 2432, 2560, 2688
COL_NSA_CMP = 2816
SMALL_FA = 0
SMALL_GC = 4

ROW_TILE = 512
FF_CHUNK = 256
HALO = 16

ATT_TQ = 256
ATT_TK = 512
WIN_SPAN = NSA_WINDOW + ATT_TQ
DIL_BAND = DIL_WINDOWS[0] // DIL_RATES[0]
DIL_SPAN = ATT_TQ + DIL_BAND

F32 = jnp.float32
BF16 = jnp.bfloat16


def _alibi_slopes_np(n):
    return np.exp2(-8.0 * np.arange(1, n + 1, dtype=np.float64) / n)


_SLOPES = _alibi_slopes_np(ALIBI_HEADS)
DIL_SLOPES = [float(np.float32(s)) for s in _SLOPES[0::2]]
NSA_SLOPES = [float(np.float32(s)) for s in _SLOPES[1::2]]


def _cparams(sem):
    return pltpu.CompilerParams(dimension_semantics=sem, vmem_limit_bytes=VMEM_LIMIT_BYTES)


def _rms(x, g):
    return x * lax.rsqrt(jnp.mean(x * x, axis=-1, keepdims=True) + RMS_EPS) * g


def _gelu_tanh(x):
    return 0.5 * x * (1.0 + jnp.tanh(np.float32(np.sqrt(2.0 / np.pi)) * (x + 0.044715 * (x * x * x))))


def _dot(a, b):
    return jnp.dot(a, b, preferred_element_type=F32)


def _dot_nt(a, b):
    return lax.dot_general(a, b, (((1,), (1,)), ((), ())), preferred_element_type=F32)


def _split3(x):
    hi = x.astype(BF16)
    r = x - hi.astype(F32)
    mid = r.astype(BF16)
    lo = (r - mid.astype(F32)).astype(BF16)
    return hi, mid, lo


def _inproj_kernel(x_ref, g_ref, w_ref, b_ref, ws_ref, bs_ref, qkv_ref, small_ref):
    h = _rms(x_ref[...], g_ref[...]).astype(BF16)
    for c in range(0, QKV_W, 512):
        qkv_ref[:, c:c + 512] = (_dot(h, w_ref[:, c:c + 512]) + b_ref[:, c:c + 512]).astype(BF16)
    small_ref[...] = _dot(h, ws_ref[...]) + bs_ref[...]


def _inproj(x2, g, w, b, ws, bs):
    t = x2.shape[0]
    const = lambda i: (0, 0)
    return pl.pallas_call(
        _inproj_kernel,
        out_shape=(jax.ShapeDtypeStruct((t, QKV_W), BF16), jax.ShapeDtypeStruct((t, LANES), F32)),
        grid=(t // ROW_TILE,),
        in_specs=[
            pl.BlockSpec((ROW_TILE, D_MODEL), lambda i: (i, 0)),
            pl.BlockSpec((1, D_MODEL), const),
            pl.BlockSpec((D_MODEL, QKV_W), const),
            pl.BlockSpec((1, QKV_W), const),
            pl.BlockSpec((D_MODEL, LANES), const),
            pl.BlockSpec((1, LANES), const),
        ],
        out_specs=(pl.BlockSpec((ROW_TILE, QKV_W), lambda i: (i, 0)),
                   pl.BlockSpec((ROW_TILE, LANES), lambda i: (i, 0))),
        compiler_params=_cparams(("parallel",)),
        name="inproj",
    )(x2, g, w, b, ws, bs)


def _own_and_ones(x, low_half):
    lane = lax.broadcasted_iota(jnp.int32, x.shape, 1)
    own = (lane < HEAD_DIM) if low_half else (lane >= HEAD_DIM)
    return jnp.where(own, x, jnp.ones_like(x))


def _own_and_zeros(x, low_half):
    lane = lax.broadcasted_iota(jnp.int32, x.shape, 1)
    own = (lane < HEAD_DIM) if low_half else (lane >= HEAD_DIM)
    return jnp.where(own, x, jnp.zeros_like(x))


def _softmax_step(chunks, off, m_old, v_tile):
    mx = chunks[0]
    for sc in chunks[1:]:
        mx = jnp.maximum(mx, sc)
    m_new = jnp.maximum(m_old, jnp.max(mx, axis=1, keepdims=True) + off)
    shift = m_new - off
    p = jnp.concatenate([jnp.exp(sc - shift).astype(BF16) for sc in chunks], axis=1)
    return m_new, jnp.exp(m_old - m_new), _dot(p, v_tile)


def _diag_tile(tile, qi, row0):
    assert ATT_TK == 2 * ATT_TQ

    @pl.when(qi % 2 == 0)
    def _():
        tile(row0, ATT_TQ, True)

    @pl.when(qi % 2 == 1)
    def _():
        tile(row0 - ATT_TQ, ATT_TK, True)


def _normalise(acc):
    return acc / jnp.maximum(pltpu.roll(acc, HEAD_DIM, 1), TINY)


FOX_XROWS = 16


def _rows_to_lanes(x):
    full = jnp.concatenate([x, jnp.zeros((LANES - FOX_XROWS, SEQ), F32)], axis=0)
    return full.T.astype(BF16)


def _fox_kernel(q_ref, k_ref, v_ref, small_ref, o_ref, qaug_ref, kaug_ref, vaug_ref, qst_ref, m_ref, acc_ref,
                sa_ref, sb_ref):
    qi = pl.program_id(1)
    tq, tk = ATT_TQ, ATT_TK
    nslot = FOX_HEADS // 2

    @pl.when(qi == 0)
    def _():
        fa = small_ref[...].T[0:8, :]
        ls = jnp.minimum(fa, 0.0) - jnp.log(1.0 + jnp.exp(-jnp.abs(fa)))
        pos = lax.broadcasted_iota(jnp.int32, ls.shape, 1)
        c = ls
        sh = 1
        while sh < SEQ:
            c = c + jnp.where(pos >= sh, pltpu.roll(c, sh, 1), 0.0)
            sh *= 2
        pieces = [p.astype(F32) for p in _split3(c)]
        xrow = lax.broadcasted_iota(jnp.int32, (FOX_XROWS, SEQ), 0)
        for j in range(nslot):
            kx = jnp.where((xrow >= 6) & (xrow < 12), 1.0, 0.0)
            for e in range(2):
                h = 2 * j + e
                qx = jnp.where((xrow >= 3 * e) & (xrow < 3 * e + 3), -1.0, 0.0)
                for i, piece in enumerate(pieces):
                    kx = jnp.where(xrow == 3 * e + i, piece[h:h + 1, :], kx)
                    qx = jnp.where(xrow == 6 + 3 * e + i, piece[h:h + 1, :], qx)
                qaug_ref[h] = _rows_to_lanes(qx)
            kaug_ref[j] = _rows_to_lanes(kx)
            v2 = v_ref[:, j * LANES:(j + 1) * LANES]
            vaug_ref[2 * j] = _own_and_ones(v2, True)
            vaug_ref[2 * j + 1] = _own_and_ones(v2, False)

    row0 = pl.multiple_of(qi * tq, tq)
    scale = jnp.asarray(HEAD_DIM ** -0.5, BF16)
    lane = lax.broadcasted_iota(jnp.int32, (tq, LANES), 1)
    rloc = lax.broadcasted_iota(jnp.int32, (tq, LANES), 0)
    for j in range(nslot):
        q2 = q_ref[:, j * LANES:(j + 1) * LANES] * scale
        for e in range(2):
            qst_ref[j, e * tq:(e + 1) * tq, 0:LANES] = _own_and_zeros(q2, e == 0)
            qst_ref[j, e * tq:(e + 1) * tq, LANES:2 * LANES] = qaug_ref[2 * j + e, pl.ds(row0, tq), :]
    m_ref[...] = jnp.full(m_ref.shape, NEG_INF, F32)
    acc_ref[...] = jnp.zeros(acc_ref.shape, F32)

    def qk(start, s_ref):
        start = pl.multiple_of(start, tq)
        for j in range(nslot):
            keys = jnp.concatenate([k_ref[pl.ds(start, tk), j * LANES:(j + 1) * LANES],
                                    kaug_ref[j, pl.ds(start, tk), :]], axis=1)
            s_ref[j] = _dot_nt(qst_ref[j], keys)

    def soft(start, s_ref, diag):
        start = pl.multiple_of(start, tq)
        for j in range(nslot):
            for e in range(2):
                h = 2 * j + e
                chunks = []
                for ck in range(tk // LANES):
                    sc = s_ref[j, e * tq:(e + 1) * tq, ck * LANES:(ck + 1) * LANES]
                    if diag:
                        sc = jnp.where((lane - rloc) <= (row0 - start - ck * LANES), sc, NEG_INF)
                    chunks.append(sc)
                m_new, alpha, pv = _softmax_step(chunks, 0.0, m_ref[h], vaug_ref[h, pl.ds(start, tk), :])
                m_ref[h] = m_new
                acc_ref[h] = alpha * acc_ref[h] + pv

    nfull = (qi * tq) // tk
    qk(0, sa_ref)

    def pair(i, carry):
        qk((2 * i + 1) * tk, sb_ref)
        soft((2 * i) * tk, sa_ref, False)
        qk((2 * i + 2) * tk, sa_ref)
        soft((2 * i + 1) * tk, sb_ref, False)
        return carry

    lax.fori_loop(0, nfull // 2, pair, 0)

    @pl.when(nfull % 2 == 1)
    def _():
        qk(nfull * tk, sb_ref)
        soft((nfull - 1) * tk, sa_ref, False)
        soft(nfull * tk, sb_ref, True)

    @pl.when(nfull % 2 == 0)
    def _():
        soft(nfull * tk, sa_ref, True)

    for j in range(nslot):
        o_ref[:, j * LANES:(j + 1) * LANES] = jnp.where(
            lane < HEAD_DIM, _normalise(acc_ref[2 * j]), _normalise(acc_ref[2 * j + 1])).astype(BF16)


def _fox(qkv3, small3):
    b = qkv3.shape[0]
    return pl.pallas_call(
        _fox_kernel,
        out_shape=jax.ShapeDtypeStruct((b, SEQ, FOX_W), BF16),
        grid=(b, SEQ // ATT_TQ),
        in_specs=[
            pl.BlockSpec((None, ATT_TQ, FOX_W), lambda bi, qi: (bi, qi, COL_FOX_Q // FOX_W)),
            pl.BlockSpec((None, SEQ, FOX_W), lambda bi, qi: (bi, 0, COL_FOX_K // FOX_W)),
            pl.BlockSpec((None, SEQ, FOX_W), lambda bi, qi: (bi, 0, COL_FOX_V // FOX_W)),
            pl.BlockSpec((None, SEQ, LANES), lambda bi, qi: (bi, 0, 0)),
        ],
        out_specs=pl.BlockSpec((None, ATT_TQ, FOX_W), lambda bi, qi: (bi, qi, 0)),
        scratch_shapes=[
            pltpu.VMEM((FOX_HEADS, SEQ, LANES), BF16),
            pltpu.VMEM((FOX_HEADS // 2, SEQ, LANES), BF16),
            pltpu.VMEM((FOX_HEADS, SEQ, LANES), BF16),
            pltpu.VMEM((FOX_HEADS // 2, 2 * ATT_TQ, 2 * LANES), BF16),
            pltpu.VMEM((FOX_HEADS, ATT_TQ, LANES), F32),
            pltpu.VMEM((FOX_HEADS, ATT_TQ, LANES), F32),
            pltpu.VMEM((FOX_HEADS // 2, 2 * ATT_TQ, ATT_TK), F32),
            pltpu.VMEM((FOX_HEADS // 2, 2 * ATT_TQ, ATT_TK), F32),
        ],
        compiler_params=_cparams(("parallel", "arbitrary")),
        name="fox_attention",
    )(qkv3, qkv3, qkv3, small3)


def _dil_bias(g, e, delta):
    tq = ATT_TQ
    r = DIL_RATES[g]
    n = SEQ // r
    span = DIL_SPAN if n > tq else tq
    row = lax.broadcasted_iota(jnp.int32, (tq, span), 0)
    col = lax.broadcasted_iota(jnp.int32, (tq, span), 1)
    d = (row - col) + delta
    ok = (d >= 0) & (d <= DIL_BAND)
    if n <= tq:
        ok = ok & ((row // n) == (col // n))
    return jnp.where(ok, (-DIL_SLOPES[2 * g + e] * r) * d.astype(F32), NEG_INF)


def _dil_kernel(q_ref, k_ref, v_ref, o_ref, stage_ref, qd_ref, kd_ref, vaug_ref, out32_ref, lse_ref,
                bias_ref, bias_last_ref):
    tq = ATT_TQ
    ngroup = len(DIL_RATES)
    scale = jnp.asarray(HEAD_DIM ** -0.5, BF16)

    @pl.when(pl.program_id(0) == 0)
    def _():
        for g in range(ngroup - 1):
            for e in range(2):
                bias_ref[g, e, 0] = _dil_bias(g, e, 0)
                bias_ref[g, e, 1] = _dil_bias(g, e, DIL_BAND)
        for e in range(2):
            bias_last_ref[e] = _dil_bias(ngroup - 1, e, 0)

    for g, r in enumerate(DIL_RATES):
        n = SEQ // r
        cols = slice(g * LANES, (g + 1) * LANES)
        for which, src in enumerate((q_ref, k_ref, v_ref)):
            x = src[:, cols]
            if which == 0:
                x = x * scale
            if r > 1:
                stage_ref[...] = x.astype(F32)
                x = jnp.concatenate([stage_ref[pl.ds(rho, n, stride=r), :] for rho in range(r)], axis=0).astype(BF16)
            if which == 0:
                qd_ref[g, 0] = _own_and_zeros(x, True)
                qd_ref[g, 1] = _own_and_zeros(x, False)
            elif which == 1:
                kd_ref[g] = x
            else:
                vaug_ref[g, 0] = _own_and_ones(x, True)
                vaug_ref[g, 1] = _own_and_ones(x, False)

    low = lax.broadcasted_iota(jnp.int32, (tq, LANES), 1) < HEAD_DIM
    assert all(SEQ // r > tq for r in DIL_RATES[:-1]) and SEQ // DIL_RATES[-1] <= tq

    def tile(i):
        u0 = pl.multiple_of(i * tq, tq)
        for g, r in enumerate(DIL_RATES):
            n = SEQ // r
            span = DIL_SPAN if n > tq else tq
            if n > tq:
                cstart = (u0 // n) * n
                kstart = pl.multiple_of(cstart + jnp.maximum(u0 - cstart - DIL_BAND, 0), LANES)
            else:
                kstart = u0
            later = jnp.minimum(u0 - kstart, 1)
            k2 = kd_ref[g, pl.ds(kstart, span), :]
            halves, lses = [], []
            for e in range(2):
                s = _dot_nt(qd_ref[g, e, pl.ds(u0, tq), :], k2)
                chunks = []
                for ck in range(span // LANES):
                    cols = slice(ck * LANES, (ck + 1) * LANES)
                    bias = bias_ref[g, e, later, :, cols] if n > tq else bias_last_ref[e, :, cols]
                    chunks.append(s[:, cols] + bias)
                mx = chunks[0]
                for sc in chunks[1:]:
                    mx = jnp.maximum(mx, sc)
                m = jnp.broadcast_to(jnp.max(mx, axis=1, keepdims=True), (tq, LANES))
                p = jnp.concatenate([jnp.exp(sc - m).astype(BF16) for sc in chunks], axis=1)
                pv = _dot(p, vaug_ref[g, e, pl.ds(kstart, span), :])
                l = jnp.maximum(pltpu.roll(pv, HEAD_DIM, 1), TINY)
                halves.append(pv / l)
                lses.append(m + jnp.log(l))
            o_slot = jnp.where(low, halves[0], halves[1])
            lse_slot = jnp.where(low, lses[0], lses[1])
            if r == 1:
                out32_ref[g, pl.ds(u0, tq), :] = o_slot
                lse_ref[g, pl.ds(u0, tq), :] = lse_slot
            elif n > tq:
                rho = u0 // n
                t0 = (u0 - rho * n) * r + rho
                out32_ref[g, pl.ds(t0, tq, stride=r), :] = o_slot
                lse_ref[g, pl.ds(t0, tq, stride=r), :] = lse_slot
            else:
                for cc in range(tq // n):
                    rho = u0 // n + cc
                    out32_ref[g, pl.ds(rho, n, stride=r), :] = o_slot[cc * n:(cc + 1) * n, :]
                    lse_ref[g, pl.ds(rho, n, stride=r), :] = lse_slot[cc * n:(cc + 1) * n, :]

    def two_tiles(i, carry):
        tile(2 * i)
        tile(2 * i + 1)
        return carry

    lax.fori_loop(0, SEQ // (2 * tq), two_tiles, 0)

    top = jnp.maximum(jnp.maximum(lse_ref[0], lse_ref[1]), lse_ref[2])
    ws = [jnp.exp(lse_ref[g] - top) for g in range(ngroup)]
    den = ws[0] + ws[1] + ws[2]
    for g in range(ngroup):
        o_ref[:, g * LANES:(g + 1) * LANES] = (out32_ref[g] * (ws[g] / den)).astype(BF16)


def _dil(qkv3):
    b = qkv3.shape[0]
    ngroup = len(DIL_RATES)
    blk = lambda col: pl.BlockSpec((None, SEQ, DIL_W), lambda bi: (bi, 0, col // DIL_W))
    return pl.pallas_call(
        _dil_kernel,
        out_shape=jax.ShapeDtypeStruct((b, SEQ, DIL_W), BF16),
        grid=(b,),
        in_specs=[blk(COL_DIL_Q), blk(COL_DIL_K), blk(COL_DIL_V)],
        out_specs=pl.BlockSpec((None, SEQ, DIL_W), lambda bi: (bi, 0, 0)),
        scratch_shapes=[
            pltpu.VMEM((SEQ, LANES), F32),
            pltpu.VMEM((ngroup, 2, SEQ, LANES), BF16),
            pltpu.VMEM((ngroup, SEQ, LANES), BF16),
            pltpu.VMEM((ngroup, 2, SEQ, LANES), BF16),
            pltpu.VMEM((ngroup, SEQ, LANES), F32),
            pltpu.VMEM((ngroup, SEQ, LANES), F32),
            pltpu.VMEM((ngroup - 1, 2, 2, ATT_TQ, DIL_SPAN), F32),
            pltpu.VMEM((2, ATT_TQ, ATT_TQ), F32),
        ],
        compiler_params=_cparams(("arbitrary",)),
        name="dilated_attention",
    )(qkv3, qkv3, qkv3)


PE_ROWS = 16


def _cmp_kernel(x_ref, pe_ref, w1ak_ref, w1bk_ref, w2k_ref, w1av_ref, w1bv_ref, w2v_ref, kc_ref, vc_ref,
                stage_ref, xp_ref):
    xp_ref[N_CMP_PAD:, :] = pe_ref[...]
    for kind, (w1a_ref, w1b_ref, w2_ref, out_ref) in enumerate(
            ((w1ak_ref, w1bk_ref, w2k_ref, kc_ref), (w1av_ref, w1bv_ref, w2v_ref, vc_ref))):
        stage_ref[...] = x_ref[:, kind * LANES:(kind + 1) * LANES].astype(F32)
        for m in range(NSA_CMP_STRIDE):
            xp_ref[0:N_CMP_PAD, m * LANES:(m + 1) * LANES] = stage_ref[
                pl.ds(m, N_CMP_PAD, stride=NSA_CMP_STRIDE), :].astype(BF16)
        xp = xp_ref[...]
        a = _dot(xp, w1a_ref[...])
        b = _dot(xp, w1b_ref[...])
        cpe = a[N_CMP_PAD:N_CMP_PAD + 1, :] + b[N_CMP_PAD + 1:N_CMP_PAD + 2, :]
        bnext = pltpu.roll(b[0:N_CMP_PAD, :], N_CMP_PAD - 1, 0)
        h1 = _gelu_tanh(a[0:N_CMP_PAD, :] + bnext + cpe)
        out_ref[...] = _dot(h1.astype(BF16), w2_ref[...]).astype(BF16)


def _compress(qkv3, pe2, w1ak, w1bk, w2k, w1av, w1bv, w2v):
    b = qkv3.shape[0]
    whole = lambda a: pl.BlockSpec(a.shape, lambda bi: (0, 0))
    cmp_w = 2 * NSA_KV_W
    return pl.pallas_call(
        _cmp_kernel,
        out_shape=(jax.ShapeDtypeStruct((b, N_CMP_PAD, LANES), BF16),) * 2,
        grid=(b,),
        in_specs=[pl.BlockSpec((None, SEQ, cmp_w), lambda bi: (bi, 0, COL_NSA_CMP // cmp_w)),
                  whole(pe2), whole(w1ak), whole(w1bk), whole(w2k), whole(w1av), whole(w1bv), whole(w2v)],
        out_specs=(pl.BlockSpec((None, N_CMP_PAD, LANES), lambda bi: (bi, 0, 0)),) * 2,
        scratch_shapes=[pltpu.VMEM((SEQ, LANES), F32),
                        pltpu.VMEM((N_CMP_PAD + PE_ROWS, NSA_CMP_STRIDE * LANES), BF16)],
        compiler_params=_cparams(("parallel",)),
        name="nsa_compress",
    )(qkv3, pe2, w1ak, w1bk, w2k, w1av, w1bv, w2v)


def _nsa_head(j, e):
    return j + (NSA_HEADS // NSA_KV_GROUPS) * e


XL_POS = 2 * N_SEL


def _nsa_kernel(q_ref, ksl_ref, vsl_ref, kwn_ref, vwn_ref, kc_ref, vc_ref, small_ref,
                eg_ref, ovt_ref, kx_ref, qx_ref, o_ref,
                vaug_ref, qst_ref, score_ref, selp_ref, m_ref, acc_ref):
    qi = pl.program_id(1)
    tq, tk = ATT_TQ, ATT_TK
    nhs = NSA_HEADS
    row0 = pl.multiple_of(qi * tq, tq)
    lane = lax.broadcasted_iota(jnp.int32, (tq, LANES), 1)
    rloc = lax.broadcasted_iota(jnp.int32, (tq, LANES), 0)
    low = lane < HEAD_DIM
    slope_of = [NSA_SLOPES[_nsa_head(hs // 2, hs % 2)] for hs in range(nhs)]

    @pl.when(qi == 0)
    def _():
        for i, src in enumerate((vsl_ref, vwn_ref)):
            vaug_ref[2 * i] = _own_and_ones(src[...], True)
            vaug_ref[2 * i + 1] = _own_and_ones(src[...], False)

    scale = jnp.asarray(HEAD_DIM ** -0.5, BF16)
    for j in range(nhs // 2):
        q2 = q_ref[:, j * LANES:(j + 1) * LANES] * scale
        for e in range(2):
            hs = 2 * j + e
            qst_ref[hs * tq:(hs + 1) * tq, 0:LANES] = _own_and_zeros(q2, e == 0)
            qst_ref[hs * tq:(hs + 1) * tq, LANES:2 * LANES] = jnp.broadcast_to(
                qx_ref[hs:hs + 1, :], (tq, LANES)).astype(BF16)

    sg = jax.nn.sigmoid(small_ref[...])
    ghi = sg.astype(BF16)
    glo = (sg - ghi.astype(F32)).astype(BF16)
    gexp = _dot(ghi, eg_ref[...]) + _dot(glo, eg_ref[...])

    tpos = row0 + rloc
    tposf = tpos.astype(F32)
    row_off = [-slope_of[hs] * tposf for hs in range(nhs)]

    wstart = pl.multiple_of(jnp.maximum(row0 - NSA_WINDOW, 0), tq)
    delta_w = row0 - wstart
    keys_w = jnp.concatenate([kwn_ref[pl.ds(wstart, WIN_SPAN), :], kx_ref[pl.ds(wstart, WIN_SPAN), :]], axis=1)
    s_w = _dot_nt(qst_ref[...], keys_w)
    in_window = []
    for ck in range(WIN_SPAN // LANES):
        d = (rloc - lane) + (delta_w - ck * LANES)
        in_window.append(jnp.where((d >= 0) & (d < NSA_WINDOW), 0.0, NEG_INF))
    o_win = []
    for hs in range(nhs):
        chunks = [s_w[hs * tq:(hs + 1) * tq, ck * LANES:(ck + 1) * LANES] + in_window[ck]
                  for ck in range(WIN_SPAN // LANES)]
        _, _, pv = _softmax_step(chunks, row_off[hs], jnp.full((tq, LANES), NEG_INF, F32),
                                 vaug_ref[2 + hs % 2, pl.ds(wstart, WIN_SPAN), :])
        o_win.append(pv)

    dist_c = tpos - (lane * NSA_CMP_STRIDE + NSA_CMP_BLOCK - 1)
    valid_c = dist_c >= 0
    dist_cf = dist_c.astype(F32)
    s_c = _dot_nt(qst_ref[:, 0:LANES], kc_ref[...])
    psum = [jnp.zeros((tq, LANES), F32) for _ in range(NSA_KV_GROUPS)]
    o_cmp = []
    for hs in range(nhs):
        s = s_c[hs * tq:(hs + 1) * tq, :] - slope_of[hs] * dist_cf
        s = jnp.where(valid_c, s, NEG_INF)
        m = jnp.max(s, axis=1, keepdims=True)
        p = jnp.where(valid_c, jnp.exp(s - m), 0.0)
        p = p / jnp.maximum(jnp.sum(p, axis=1, keepdims=True), TINY)
        psum[hs % 2] = psum[hs % 2] + p
        o_cmp.append(_dot(p.astype(BF16), vc_ref[...]))

    jrow = lax.broadcasted_iota(jnp.int32, (N_SEL, tq), 0)
    qblk = (row0 + lax.broadcasted_iota(jnp.int32, (N_SEL, tq), 1)) >> 6
    causal = jrow <= qblk
    forced = (jrow == 0) | (jrow == qblk) | (jrow == qblk - 1)
    selp_ref[NSA_KV_GROUPS * N_SEL:, :] = jnp.zeros((LANES - NSA_KV_GROUPS * N_SEL, tq), F32)
    for g in range(NSA_KV_GROUPS):
        imp = sum(_dot_nt(ovt_ref[...], part) for part in _split3(psum[g]))
        score = jnp.where(causal & forced, FORCE_SCORE, jnp.where(causal, imp, -1.0))
        score_ref[g] = score
        rank = jnp.zeros((N_SEL, tq), jnp.int32)
        for jp in range(N_SEL):
            other = score_ref[g, jp:jp + 1, :]
            ahead = (other > score) | ((other == score) & (jp < jrow))
            rank = rank + ahead.astype(jnp.int32)
        sel = (rank < NSA_SEL_TOPK) & causal
        selp_ref[g * N_SEL:(g + 1) * N_SEL, :] = jnp.where(sel, 0.0, NEG_INF)
    sel_t = selp_ref[...].T
    for hs in range(nhs):
        g = hs % 2
        mine = (lane >= g * N_SEL) & (lane < (g + 1) * N_SEL)
        extra = jnp.where(mine, sel_t, 0.0) + qx_ref[hs:hs + 1, :]
        qst_ref[hs * tq:(hs + 1) * tq, LANES:2 * LANES] = extra.astype(BF16)

    m_ref[...] = jnp.full(m_ref.shape, NEG_INF, F32)
    acc_ref[...] = jnp.zeros(acc_ref.shape, F32)

    def sel_tile(start, width, diag):
        start = pl.multiple_of(start, tq)
        keys = jnp.concatenate([ksl_ref[pl.ds(start, width), :], kx_ref[pl.ds(start, width), :]], axis=1)
        s_all = _dot_nt(qst_ref[...], keys)
        for hs in range(nhs):
            chunks = []
            for ck in range(width // LANES):
                sc = s_all[hs * tq:(hs + 1) * tq, ck * LANES:(ck + 1) * LANES]
                if diag:
                    sc = jnp.where((lane - rloc) <= (row0 - start - ck * LANES), sc, NEG_INF)
                chunks.append(sc)
            m_new, alpha, pv = _softmax_step(chunks, row_off[hs], m_ref[hs], vaug_ref[hs % 2, pl.ds(start, width), :])
            m_ref[hs] = m_new
            acc_ref[hs] = alpha * acc_ref[hs] + pv

    nfull = (qi * tq) // tk

    def body(i, carry):
        sel_tile((nfull - 1 - i) * tk, tk, False)
        return carry

    _diag_tile(sel_tile, qi, row0)
    lax.fori_loop(0, nfull, body, 0)

    for j in range(nhs // 2):
        oc = jnp.where(low, o_cmp[2 * j], o_cmp[2 * j + 1])
        osl = jnp.where(low, _normalise(acc_ref[2 * j]), _normalise(acc_ref[2 * j + 1]))
        ow = jnp.where(low, _normalise(o_win[2 * j]), _normalise(o_win[2 * j + 1]))
        g_c = gexp[:, 0 * NSA_W + j * LANES:0 * NSA_W + (j + 1) * LANES]
        g_s = gexp[:, 1 * NSA_W + j * LANES:1 * NSA_W + (j + 1) * LANES]
        g_w = gexp[:, 2 * NSA_W + j * LANES:2 * NSA_W + (j + 1) * LANES]
        o_ref[:, j * LANES:(j + 1) * LANES] = (g_c * oc + g_s * osl + g_w * ow).astype(BF16)


def _nsa(qkv3, kc2, vc2, small3, eg, ovt, kx, qx):
    b = qkv3.shape[0]
    full = lambda col: pl.BlockSpec((None, SEQ, LANES), lambda bi, qi: (bi, 0, col // LANES))
    c2 = lambda bi, qi: (0, 0)
    return pl.pallas_call(
        _nsa_kernel,
        out_shape=jax.ShapeDtypeStruct((b, SEQ, NSA_W), BF16),
        grid=(b, SEQ // ATT_TQ),
        in_specs=[
            pl.BlockSpec((None, ATT_TQ, NSA_W), lambda bi, qi: (bi, qi, COL_NSA_Q // NSA_W)),
            full(COL_NSA_KSEL), full(COL_NSA_VSEL), full(COL_NSA_KWIN), full(COL_NSA_VWIN),
            pl.BlockSpec((None, N_CMP_PAD, LANES), lambda bi, qi: (bi, 0, 0)),
            pl.BlockSpec((None, N_CMP_PAD, LANES), lambda bi, qi: (bi, 0, 0)),
            pl.BlockSpec((None, ATT_TQ, LANES), lambda bi, qi: (bi, qi, 0)),
            pl.BlockSpec((LANES, 3 * NSA_W), c2),
            pl.BlockSpec((N_SEL, LANES), c2),
            pl.BlockSpec((SEQ, LANES), c2),
            pl.BlockSpec((8, LANES), c2),
        ],
        out_specs=pl.BlockSpec((None, ATT_TQ, NSA_W), lambda bi, qi: (bi, qi, 0)),
        scratch_shapes=[
            pltpu.VMEM((4, SEQ, LANES), BF16),
            pltpu.VMEM((NSA_HEADS * ATT_TQ, 2 * LANES), BF16),
            pltpu.VMEM((NSA_KV_GROUPS, N_SEL, ATT_TQ), F32),
            pltpu.VMEM((LANES, ATT_TQ), F32),
            pltpu.VMEM((NSA_HEADS, ATT_TQ, LANES), F32),
            pltpu.VMEM((NSA_HEADS, ATT_TQ, LANES), F32),
        ],
        compiler_params=_cparams(("parallel", "arbitrary")),
        name="nsa_attention",
    )(qkv3, qkv3, qkv3, qkv3, qkv3, kc2, vc2, small3, eg, ovt, kx, qx)


def _merge_kernel(x_ref, g_ref, ya_ref, yb_ref, yc_ref, wg_ref, bg_ref, wpa_ref, wpb_ref, wpc_ref, wo_ref, o_ref):
    x = x_ref[...]
    h = _rms(x, g_ref[...]).astype(BF16)
    merged = None
    for i, (y_ref, wp_ref) in enumerate(((ya_ref, wpa_ref), (yb_ref, wpb_ref), (yc_ref, wpc_ref))):
        cols = slice(i * D_MODEL, (i + 1) * D_MODEL)
        gate = jax.nn.sigmoid(_dot(h, wg_ref[:, cols]) + bg_ref[:, cols])
        term = gate * _dot(y_ref[...], wp_ref[...])
        merged = term if merged is None else merged + term
    o_ref[...] = x + _dot(merged.astype(BF16), wo_ref[...])


def _merge(x2, g, ya, yb, yc, wg, bg, wpa, wpb, wpc, wo):
    t = x2.shape[0]
    const = lambda i: (0, 0)
    rows = lambda w: pl.BlockSpec((ROW_TILE, w), lambda i: (i, 0))
    whole = lambda a: pl.BlockSpec(a.shape, const)
    return pl.pallas_call(
        _merge_kernel,
        out_shape=jax.ShapeDtypeStruct((t, D_MODEL), F32),
        grid=(t // ROW_TILE,),
        in_specs=[rows(D_MODEL), whole(g), rows(FOX_W), rows(DIL_W), rows(NSA_W),
                  whole(wg), whole(bg), whole(wpa), whole(wpb), whole(wpc), whole(wo)],
        out_specs=rows(D_MODEL),
        compiler_params=_cparams(("parallel",)),
        name="merge_oproj",
    )(x2, g, ya, yb, yc, wg, bg, wpa, wpb, wpc, wo)


def _ffn_kernel(x_ref, xh_ref, g_ref, wup_ref, cw_ref, cb_ref, wdn_ref, fg_ref, o_ref, h_ref, act_ref, *, final_norm):
    i = pl.program_id(0)
    x = x_ref[...]
    g = g_ref[...]
    h_ref[0:HALO, :] = _rms(xh_ref[...], g).astype(BF16)
    h_ref[HALO:, :] = _rms(x, g).astype(BF16)
    row = lax.broadcasted_iota(jnp.int32, (ROW_TILE + HALO, FF_CHUNK), 0)
    seq_start = (i % (SEQ // ROW_TILE)) == 0
    keep = jnp.logical_not((row < HALO) & seq_start)
    for c in range(0, D_FF, FF_CHUNK):
        a = jnp.where(keep, _dot(h_ref[...], wup_ref[:, c:c + FF_CHUNK]), 0.0)
        conv = (a * cw_ref[2:3, c:c + FF_CHUNK]
                + pltpu.roll(a, 1, 0) * cw_ref[1:2, c:c + FF_CHUNK]
                + pltpu.roll(a, 2, 0) * cw_ref[0:1, c:c + FF_CHUNK]
                + cb_ref[:, c:c + FF_CHUNK])
        gate = _dot(h_ref[HALO:, :], wup_ref[:, D_FF + c:D_FF + c + FF_CHUNK])
        act_ref[:, c:c + FF_CHUNK] = (_gelu_tanh(conv[HALO:, :]) * gate).astype(BF16)
    y = x + _dot(act_ref[...], wdn_ref[...])
    if final_norm:
        y = _rms(y, fg_ref[...])
    o_ref[...] = y


def _ffn(x2, g, wup, cw, cb, wdn, fg, final_norm):
    t = x2.shape[0]
    const = lambda i: (0, 0)
    whole = lambda a: pl.BlockSpec(a.shape, const)
    halo_blocks = ROW_TILE // HALO
    return pl.pallas_call(
        functools.partial(_ffn_kernel, final_norm=final_norm),
        out_shape=jax.ShapeDtypeStruct((t, D_MODEL), F32),
        grid=(t // ROW_TILE,),
        in_specs=[
            pl.BlockSpec((ROW_TILE, D_MODEL), lambda i: (i, 0)),
            pl.BlockSpec((HALO, D_MODEL), lambda i: (jnp.maximum(i * halo_blocks - 1, 0), 0)),
            whole(g),
            pl.BlockSpec(wup.shape, const, pipeline_mode=pl.Buffered(1)),
            whole(cw), whole(cb),
            pl.BlockSpec(wdn.shape, const, pipeline_mode=pl.Buffered(1)),
            whole(fg),
        ],
        out_specs=pl.BlockSpec((ROW_TILE, D_MODEL), lambda i: (i, 0)),
        scratch_shapes=[pltpu.VMEM((ROW_TILE + HALO, D_MODEL), BF16), pltpu.VMEM((ROW_TILE, D_FF), BF16)],
        compiler_params=_cparams(("parallel",)),
        name="conv_ffn",
    )(x2, x2, g, wup, cw, cb, wdn, fg)


def _nsa_slot_perm():
    cols = []
    for j in range(NSA_HEADS // 2):
        for e in range(2):
            h = _nsa_head(j, e)
            cols += list(range(h * HEAD_DIM, (h + 1) * HEAD_DIM))
    return np.asarray(cols)


def _constants():
    perm = _nsa_slot_perm()
    eg = np.zeros((LANES, 3 * NSA_W), np.float32)
    for newcol, oldcol in enumerate(perm):
        h = oldcol // HEAD_DIM
        for br in range(3):
            eg[SMALL_GC + h * 3 + br, br * NSA_W + newcol] = 1.0
    ovt = np.zeros((N_SEL, LANES), np.float32)
    n_cmp = (SEQ - NSA_CMP_BLOCK) // NSA_CMP_STRIDE + 1
    for c in range(n_cmp):
        cs, ce = c * NSA_CMP_STRIDE, c * NSA_CMP_STRIDE + NSA_CMP_BLOCK - 1
        for jb in range(N_SEL):
            if cs < jb * NSA_SEL_BLOCK + NSA_SEL_BLOCK and ce >= jb * NSA_SEL_BLOCK:
                ovt[jb, c] = 1.0
    pos = np.arange(SEQ)
    kx = np.zeros((SEQ, LANES), np.float32)
    for g in range(NSA_KV_GROUPS):
        kx[pos, g * N_SEL + pos // NSA_SEL_BLOCK] = 1.0
    kx[:, XL_POS:XL_POS + 3] = (256 * (pos // 256))[:, None]
    kx[:, XL_POS + 3:XL_POS + 6] = (pos % 256)[:, None]
    qx = np.zeros((8, LANES), np.float32)
    for hs in range(NSA_HEADS):
        s = np.float32(NSA_SLOPES[_nsa_head(hs // 2, hs % 2)])
        hi = np.float32(np.asarray(s, dtype=BF16))
        mid = np.float32(np.asarray(s - hi, dtype=BF16))
        lo = np.float32(np.asarray(s - hi - mid, dtype=BF16))
        qx[hs, XL_POS:XL_POS + 6] = [hi, mid, lo, hi, mid, lo]
    return (jnp.asarray(eg, BF16), jnp.asarray(ovt, BF16), jnp.asarray(kx, BF16), jnp.asarray(qx, F32))


def _in_offsets():
    o = np.cumsum([0, FOX_W, FOX_W, FOX_W, FOX_HEADS, DIL_W, DIL_W, DIL_W, NSA_W,
                   NSA_KV_W, NSA_KV_W, NSA_KV_W, NSA_KV_W, NSA_KV_W, NSA_KV_W, NSA_HEADS * 3, 3 * D_MODEL])
    return [int(v) for v in o]


D_IN = _in_offsets()[-1]
PACK_ROWS = 256


def _pack_kernel(w_ref, qkv_ref, gate_ref, small_ref):
    (o_qa, _, _, o_fa, o_qb, _, _, o_qc, o_kc, _, o_ks, _, _, _, o_gc, o_g, _) = _in_offsets()
    heads = [_nsa_head(j, e) for j in range(NSA_HEADS // 2) for e in range(2)]

    def put(dst, src, width):
        qkv_ref[:, dst:dst + width] = w_ref[:, src:src + width].astype(BF16)

    put(COL_FOX_Q, o_qa, 3 * FOX_W)
    put(COL_DIL_Q, o_qb, 3 * DIL_W)
    for j in range(NSA_HEADS // 2):
        pair = jnp.concatenate([w_ref[:, o_qc + h * HEAD_DIM:o_qc + (h + 1) * HEAD_DIM] for h in heads[2 * j:2 * j + 2]],
                               axis=1)
        qkv_ref[:, COL_NSA_Q + j * LANES:COL_NSA_Q + (j + 1) * LANES] = pair.astype(BF16)
    put(COL_NSA_KSEL, o_ks, 4 * NSA_KV_W)
    put(COL_NSA_CMP, o_kc, 2 * NSA_KV_W)
    gate_ref[...] = w_ref[:, o_g:o_g + 3 * D_MODEL].astype(BF16)
    pad = jnp.zeros((w_ref.shape[0], LANES - FOX_HEADS - NSA_HEADS * 3), F32)
    small_ref[...] = jnp.concatenate([w_ref[:, o_fa:o_fa + FOX_HEADS], w_ref[:, o_gc:o_g], pad], axis=1).astype(BF16)


def _pack_in_weights(w_in, l):
    rows = w_in.shape[1]
    tiled = lambda width: pl.BlockSpec((PACK_ROWS, width), lambda i: (i, 0))
    return pl.pallas_call(
        _pack_kernel,
        out_shape=(jax.ShapeDtypeStruct((rows, QKV_W), BF16), jax.ShapeDtypeStruct((rows, 3 * D_MODEL), BF16),
                   jax.ShapeDtypeStruct((rows, LANES), BF16)),
        grid=(rows // PACK_ROWS,),
        in_specs=[pl.BlockSpec((None, PACK_ROWS, D_IN), lambda i: (l, i, 0))],
        out_specs=(tiled(QKV_W), tiled(3 * D_MODEL), tiled(LANES)),
        compiler_params=_cparams(("parallel",)),
        name="pack_in_weights",
    )(w_in)


def _layer_params(l, w_in, b_in, cmp_pe, cmp_k_w1, cmp_k_w2, cmp_v_w1, cmp_v_w2, w_pc):
    (o_qa, _, _, o_fa, o_qb, _, _, o_qc, o_kc, _, o_ks, _, _, _, o_gc, o_g, _) = _in_offsets()
    b = b_in[l]
    heads = [_nsa_head(j, e) for j in range(NSA_HEADS // 2) for e in range(2)]
    pieces = ([(o_qa, o_qa + 3 * FOX_W), (o_qb, o_qb + 3 * DIL_W)]
              + [(o_qc + h * HEAD_DIM, o_qc + (h + 1) * HEAD_DIM) for h in heads]
              + [(o_ks, o_ks + 4 * NSA_KV_W), (o_kc, o_kc + 2 * NSA_KV_W)])
    w_qkv, w_gate, w_small = _pack_in_weights(w_in, l)
    pad = LANES - FOX_HEADS - NSA_HEADS * 3
    p = {
        "w_qkv": w_qkv,
        "b_qkv": jnp.concatenate([b[s:e] for s, e in pieces])[None, :],
        "w_small": w_small,
        "b_small": jnp.concatenate([b[o_fa:o_fa + FOX_HEADS], b[o_gc:o_g], jnp.zeros((pad,), b.dtype)])[None, :],
        "w_gate": w_gate,
        "b_gate": b[o_g:][None, :],
        "w_pc": jnp.concatenate([w_pc[l][h * HEAD_DIM:(h + 1) * HEAD_DIM] for h in heads], axis=0).astype(BF16),
    }
    pe = cmp_pe[l].reshape(2, NSA_CMP_STRIDE, 1, HEAD_DIM)
    pe = jnp.broadcast_to(pe, (2, NSA_CMP_STRIDE, NSA_KV_GROUPS, HEAD_DIM)).reshape(2, NSA_CMP_STRIDE * LANES)
    p["pe2"] = jnp.concatenate([pe, jnp.zeros((PE_ROWS - 2, NSA_CMP_STRIDE * LANES), pe.dtype)]).astype(BF16)
    for name, w1, w2 in (("k", cmp_k_w1[l], cmp_k_w2[l]), ("v", cmp_v_w1[l], cmp_v_w2[l])):
        for half, tag in enumerate("ab"):
            wh = w1[half * NSA_CMP_STRIDE * HEAD_DIM:(half + 1) * NSA_CMP_STRIDE * HEAD_DIM]
            wh = wh.reshape(NSA_CMP_STRIDE, HEAD_DIM, NSA_CMP_HIDDEN)
            z = jnp.zeros_like(wh)
            bd = jnp.stack([jnp.concatenate([wh, z], 2), jnp.concatenate([z, wh], 2)], axis=1)
            p["w1" + tag + name] = bd.reshape(NSA_CMP_STRIDE * LANES, NSA_KV_GROUPS * NSA_CMP_HIDDEN).astype(BF16)
        z = jnp.zeros_like(w2)
        p["w2" + name] = jnp.concatenate([jnp.concatenate([w2, z], 1), jnp.concatenate([z, w2], 1)], 0).astype(BF16)
    return p


def kernel(x, norm1_g, w_in, b_in, cmp_pe, cmp_k_w1, cmp_k_w2, cmp_v_w1, cmp_v_w2, w_pa, w_pb, w_pc, w_o,
           norm2_g, w_up, conv_w, conv_b, w_down, final_g):
    bsz, seq, d = x.shape
    assert (seq, d) == (SEQ, D_MODEL)
    depth = w_in.shape[0]
    eg, ovt, kx, qx = _constants()
    x2 = x.reshape(bsz * seq, d)
    for l in range(depth):
        p = _layer_params(l, w_in, b_in, cmp_pe, cmp_k_w1, cmp_k_w2, cmp_v_w1, cmp_v_w2, w_pc)
        g1 = norm1_g[l][None, :]
        qkv, small = _inproj(x2, g1, p["w_qkv"], p["b_qkv"], p["w_small"], p["b_small"])
        qkv3 = qkv.reshape(bsz, seq, QKV_W)
        small3 = small.reshape(bsz, seq, LANES)
        ya = _fox(qkv3, small3)
        yb = _dil(qkv3)
        kc2, vc2 = _compress(qkv3, p["pe2"], p["w1ak"], p["w1bk"], p["w2k"], p["w1av"], p["w1bv"], p["w2v"])
        yc = _nsa(qkv3, kc2, vc2, small3, eg, ovt, kx, qx)
        x2 = _merge(x2, g1, ya.reshape(-1, FOX_W), yb.reshape(-1, DIL_W), yc.reshape(-1, NSA_W),
                    p["w_gate"], p["b_gate"], w_pa[l].astype(BF16), w_pb[l].astype(BF16), p["w_pc"],
                    w_o[l].astype(BF16))
        x2 = _ffn(x2, norm2_g[l][None, :], w_up[l].astype(BF16), conv_w[l], conv_b[l][None, :],
                  w_down[l].astype(BF16), final_g[None, :], final_norm=(l == depth - 1))
    return x2.reshape(bsz, seq, d)
```
